```python
import jax, jax.numpy as jnp
from jax import lax
import numpy as np

D_MODEL = 1024
BATCH = 16
SEQ = 2048
DEPTH = 4

D_MIX = D_MODEL
D_FOURIER = D_MIX // 2
D_POOL = D_MIX - D_FOURIER
N_FOURIER_HEADS = 4
FOURIER_HEAD_DIM = D_FOURIER // N_FOURIER_HEADS
POOL_WINDOWS = (2, 4, 8, 16)
N_POOL_GROUPS = len(POOL_WINDOWS)
POOL_GROUP_DIM = D_POOL // N_POOL_GROUPS
D_FF = 128 * ((8 * D_MODEL // 3 + 127) // 128)
EPS = 1e-6

kernel_name = "hybrid_fourier_pool_macaron_encoder"


def rms_norm(x, g):
    xf = x.astype(jnp.float32)
    y = xf * lax.rsqrt(jnp.mean(xf * xf, axis=-1, keepdims=True) + EPS)
    return (y * g.astype(jnp.float32)).astype(x.dtype)


def swiglu(h, w_gate, w_up, w_down):
    a = jnp.einsum('bsd,df->bsf', h, w_gate)
    b = jnp.einsum('bsd,df->bsf', h, w_up)
    return jnp.einsum('bsf,fd->bsd', jax.nn.silu(a) * b, w_down)


def fourier_heads(u, w):
    B, S, _ = u.shape
    uh = u.reshape(B, S, N_FOURIER_HEADS, FOURIER_HEAD_DIM).astype(jnp.float32)
    f = jnp.fft.fft2(uh, axes=(1, 3), norm='ortho').real.astype(u.dtype)
    return jnp.einsum('bshc,hcd->bshd', f, w).reshape(B, S, D_FOURIER)


def centred_window_mean(u, radius):
    S = u.shape[1]
    cs = jnp.pad(jnp.cumsum(u, axis=1), ((0, 0), (1, 0), (0, 0)))
    cs = jnp.pad(cs, ((0, 0), (radius, radius), (0, 0)), mode='edge')
    win = cs[:, 2 * radius + 1:2 * radius + 1 + S] - cs[:, :S]
    t = jnp.arange(S)
    count = (jnp.minimum(t + radius, S - 1) - jnp.maximum(t - radius, 0) + 1).astype(jnp.float32)
    return win / count[None, :, None]


def pool_groups(u, w, scale):
    B, S, _ = u.shape
    uf = u.reshape(B, S, N_POOL_GROUPS, POOL_GROUP_DIM).astype(jnp.float32)
    diffs = jnp.stack(
        [centred_window_mean(uf[:, :, g], win // 2) - uf[:, :, g] for g, win in enumerate(POOL_WINDOWS)],
        axis=2).astype(u.dtype)
    y = jnp.einsum('bsgc,gcd->bsgd', diffs, w).reshape(B, S, D_POOL)
    return y * scale


def setup_inputs(seed: int = 0) -> dict:
    key = jax.random.key(seed)
    ks = jax.random.split(key, 24)

    def normal(k, shape, fan_in):
        return jax.random.normal(k, shape, jnp.float32) * (fan_in ** -0.5)

    def gain(k, shape):
        return 1.0 + 0.02 * jax.random.normal(k, shape, jnp.float32)

    L, D, F = DEPTH, D_MODEL, D_FF
    return {
        "x": jax.random.normal(ks[0], (BATCH, SEQ, D), jnp.float32),
        "ffn1_pre_g": gain(ks[1], (L, D)),
        "ffn1_w_gate": normal(ks[2], (L, D, F), D),
        "ffn1_w_up": normal(ks[3], (L, D, F), D),
        "ffn1_w_down": normal(ks[4], (L, F, D), F),
        "ffn1_post_g": gain(ks[5], (L, D)),
        "mix_pre_g": gain(ks[6], (L, D)),
        "w_in": normal(ks[7], (L, D, D_MIX), D),
        "fourier_w": normal(ks[8], (L, N_FOURIER_HEADS, FOURIER_HEAD_DIM, FOURIER_HEAD_DIM), FOURIER_HEAD_DIM),
        "pool_w": normal(ks[9], (L, N_POOL_GROUPS, POOL_GROUP_DIM, POOL_GROUP_DIM), POOL_GROUP_DIM),
        "pool_scale": gain(ks[10], (L, D_POOL)),
        "w_out": normal(ks[11], (L, D_MIX, D), D_MIX),
        "mix_post_g": gain(ks[12], (L, D)),
        "ffn2_pre_g": gain(ks[13], (L, D)),
        "ffn2_w_gate": normal(ks[14], (L, D, F), D),
        "ffn2_w_up": normal(ks[15], (L, D, F), D),
        "ffn2_w_down": normal(ks[16], (L, F, D), F),
        "ffn2_post_g": gain(ks[17], (L, D)),
    }


def reference(x, ffn1_pre_g, ffn1_w_gate, ffn1_w_up, ffn1_w_down, ffn1_post_g,
              mix_pre_g, w_in, fourier_w, pool_w, pool_scale, w_out, mix_post_g,
              ffn2_pre_g, ffn2_w_gate, ffn2_w_up, ffn2_w_down, ffn2_post_g):
    for l in range(DEPTH):
        h = rms_norm(x, ffn1_pre_g[l])
        x = x + 0.5 * rms_norm(swiglu(h, ffn1_w_gate[l], ffn1_w_up[l], ffn1_w_down[l]), ffn1_post_g[l])

        h = rms_norm(x, mix_pre_g[l])
        u = jnp.einsum('bsd,dm->bsm', h, w_in[l])
        y_fourier = fourier_heads(u[..., :D_FOURIER], fourier_w[l])
        y_pool = pool_groups(u[..., D_FOURIER:], pool_w[l], pool_scale[l])
        y = jnp.einsum('bsm,md->bsd', jnp.concatenate([y_fourier, y_pool], axis=-1), w_out[l])
        x = x + rms_norm(y, mix_post_g[l])

        h = rms_norm(x, ffn2_pre_g[l])
        x = x + 0.5 * rms_norm(swiglu(h, ffn2_w_gate[l], ffn2_w_up[l], ffn2_w_down[l]), ffn2_post_g[l])
    return x
```

```python
import functools

import numpy as np
import jax
import jax.numpy as jnp
from jax import lax
from jax.experimental import pallas as pl
from jax.experimental.pallas import tpu as pltpu

EPS = 1e-6
N_FOURIER_HEADS = 4
POOL_WINDOWS = (2, 4, 8, 16)

V7X_SUBLANES = 8
V7X_LANES = 128
V7X_VMEM_LIMIT_BYTES = 58 * 1024 * 1024

FFN_TOKEN_TILE = 512
FFN_HIDDEN_CHUNK = 256
NORM_ROW_CHUNK = 128
MIX_ROW_CHUNK = 512
MIX_OUT_ROW_TILE = 1024
POOL_PAD_ROWS = V7X_SUBLANES

BF16 = jnp.bfloat16
F32 = jnp.float32


def _rms_scale(x):
    return lax.rsqrt(jnp.mean(x * x, axis=-1, keepdims=True) + EPS)


def _ffn_body(x_ref, gpre_ref, wg_ref, wu_ref, wd_ref, gpost_ref, o_ref, h_scr, g_scr):
    tm, _ = x_ref.shape
    d_ff = wg_ref.shape[1]
    n_row_chunks = tm // NORM_ROW_CHUNK

    def pre_norm(i, carry):
        r = pl.multiple_of(i * NORM_ROW_CHUNK, NORM_ROW_CHUNK)
        x = x_ref[pl.ds(r, NORM_ROW_CHUNK), :]
        h_scr[pl.ds(r, NORM_ROW_CHUNK), :] = (x * _rms_scale(x) * gpre_ref[...]).astype(BF16)
        return carry

    lax.fori_loop(0, n_row_chunks, pre_norm, 0)

    for c in range(d_ff // FFN_HIDDEN_CHUNK):
        sl = slice(c * FFN_HIDDEN_CHUNK, (c + 1) * FFN_HIDDEN_CHUNK)
        a = jnp.dot(h_scr[...], wg_ref[:, sl], preferred_element_type=F32)
        b = jnp.dot(h_scr[...], wu_ref[:, sl], preferred_element_type=F32)
        g_scr[:, sl] = ((a / (1.0 + jnp.exp(-a))) * b).astype(BF16)

    o_ref[...] = jnp.dot(g_scr[...], wd_ref[...], preferred_element_type=F32)

    def post_norm(i, carry):
        r = pl.multiple_of(i * NORM_ROW_CHUNK, NORM_ROW_CHUNK)
        y = o_ref[pl.ds(r, NORM_ROW_CHUNK), :]
        x = x_ref[pl.ds(r, NORM_ROW_CHUNK), :]
        o_ref[pl.ds(r, NORM_ROW_CHUNK), :] = x + 0.5 * (y * _rms_scale(y) * gpost_ref[...])
        return carry

    lax.fori_loop(0, n_row_chunks, post_norm, 0)


def _ffn(x2d, gpre, wg, wu, wd, gpost, layer):
    n_tok, d = x2d.shape
    d_ff = wg.shape[-1]
    tm = FFN_TOKEN_TILE
    gain_spec = pl.BlockSpec((None, 1, d), lambda i: (layer, 0, 0))
    return pl.pallas_call(
        _ffn_body,
        out_shape=jax.ShapeDtypeStruct((n_tok, d), F32),
        grid=(n_tok // tm,),
        in_specs=[
            pl.BlockSpec((tm, d), lambda i: (i, 0)),
            gain_spec,
            pl.BlockSpec((None, d, d_ff), lambda i: (layer, 0, 0)),
            pl.BlockSpec((None, d, d_ff), lambda i: (layer, 0, 0)),
            pl.BlockSpec((None, d_ff, d), lambda i: (layer, 0, 0)),
            gain_spec,
        ],
        out_specs=pl.BlockSpec((tm, d), lambda i: (i, 0)),
        scratch_shapes=[pltpu.VMEM((tm, d), BF16), pltpu.VMEM((tm, d_ff), BF16)],
        compiler_params=pltpu.CompilerParams(
            dimension_semantics=("arbitrary",), vmem_limit_bytes=V7X_VMEM_LIMIT_BYTES),
        name="ffn",
    )(x2d, gpre, wg, wu, wd, gpost)


def _head_dft_body(cs_ref, fw_ref, o_ref):
    hd = fw_ref.shape[0]
    fw = fw_ref[...]
    o_ref[:, :hd] = jnp.dot(cs_ref[0], fw, preferred_element_type=F32,
                            precision=lax.Precision.HIGHEST).astype(BF16)
    o_ref[:, hd:] = jnp.dot(cs_ref[1], fw, preferred_element_type=F32,
                            precision=lax.Precision.HIGHEST).astype(BF16)


def _head_dft(chan_cs, fourier_w):
    n_layers, n_heads, hd, _ = fourier_w.shape
    return pl.pallas_call(
        _head_dft_body,
        out_shape=jax.ShapeDtypeStruct((n_layers, n_heads, hd, 2 * hd), BF16),
        grid=(n_layers, n_heads),
        in_specs=[
            pl.BlockSpec((2, hd, hd), lambda l, h: (0, 0, 0)),
            pl.BlockSpec((None, None, hd, hd), lambda l, h: (l, h, 0, 0)),
        ],
        out_specs=pl.BlockSpec((None, None, hd, 2 * hd), lambda l, h: (l, h, 0, 0)),
        compiler_params=pltpu.CompilerParams(dimension_semantics=("arbitrary", "arbitrary")),
        name="head_dft",
    )(chan_cs, fourier_w)


def _mix_in_body(x_ref, gpre_ref, win_ref, mcs_ref, pw_ref, pscale_ref, pq_ref, yp_ref, pad_scr):
    seq, _ = x_ref.shape
    d_f = pq_ref.shape[-1]
    hd = d_f // N_FOURIER_HEADS
    d_p = yp_ref.shape[-1]
    gd = d_p // len(POOL_WINDOWS)
    rc = MIX_ROW_CHUNK
    pad = POOL_PAD_ROWS

    zeros = jnp.zeros((pad, d_p), F32)
    pad_scr[0:pad, :] = zeros
    pad_scr[pad + seq:pad + seq + pad, :] = zeros

    for ci in range(seq // rc):
        r0 = ci * rc
        x = x_ref[r0:r0 + rc, :]
        h = (x * _rms_scale(x) * gpre_ref[...]).astype(BF16)
        u = jnp.dot(h, win_ref[...], preferred_element_type=F32)
        pad_scr[pad + r0:pad + r0 + rc, :] = u[:, d_f:]
        for hh in range(N_FOURIER_HEADS):
            uh = u[:, hh * hd:(hh + 1) * hd].astype(BF16)
            pq = jnp.dot(uh, mcs_ref[hh], preferred_element_type=F32)
            pq_ref[0, r0:r0 + rc, hh * hd:(hh + 1) * hd] = pq[:, :hd].astype(BF16)
            pq_ref[1, r0:r0 + rc, hh * hd:(hh + 1) * hd] = pq[:, hd:].astype(BF16)

    for ci in range(seq // rc):
        r0 = ci * rc
        t = r0 + lax.broadcasted_iota(jnp.int32, (rc, gd), 0)
        for g, win in enumerate(POOL_WINDOWS):
            radius = win // 2
            cols = slice(g * gd, (g + 1) * gd)
            acc = pad_scr[pad + r0 - radius:pad + r0 - radius + rc, cols]
            for j in range(-radius + 1, radius + 1):
                acc = acc + pad_scr[pad + r0 + j:pad + r0 + j + rc, cols]
            count = (jnp.minimum(t + radius, seq - 1) - jnp.maximum(t - radius, 0) + 1).astype(F32)
            diff = acc / count - pad_scr[pad + r0:pad + r0 + rc, cols]
            y = jnp.dot(diff.astype(BF16), pw_ref[g], preferred_element_type=F32)
            yp_ref[r0:r0 + rc, cols] = (y * pscale_ref[:, cols]).astype(BF16)


def _mix_in(x3d, gpre, win, mcs, pool_w, pool_scale, layer):
    n_b, seq, d = x3d.shape
    d_mix = win.shape[-1]
    n_heads, hd = mcs.shape[1], mcs.shape[2]
    d_f = n_heads * hd
    d_p = d_mix - d_f
    n_groups, gd = pool_w.shape[1], pool_w.shape[2]
    return pl.pallas_call(
        _mix_in_body,
        out_shape=(jax.ShapeDtypeStruct((n_b, 2, seq, d_f), BF16),
                   jax.ShapeDtypeStruct((n_b, seq, d_p), BF16)),
        grid=(n_b,),
        in_specs=[
            pl.BlockSpec((None, seq, d), lambda b: (b, 0, 0)),
            pl.BlockSpec((None, 1, d), lambda b: (layer, 0, 0)),
            pl.BlockSpec((None, d, d_mix), lambda b: (layer, 0, 0)),
            pl.BlockSpec((None, n_heads, hd, 2 * hd), lambda b: (layer, 0, 0, 0)),
            pl.BlockSpec((None, n_groups, gd, gd), lambda b: (layer, 0, 0, 0)),
            pl.BlockSpec((None, 1, d_p), lambda b: (layer, 0, 0)),
        ],
        out_specs=(pl.BlockSpec((None, 2, seq, d_f), lambda b: (b, 0, 0, 0)),
                   pl.BlockSpec((None, seq, d_p), lambda b: (b, 0, 0))),
        scratch_shapes=[pltpu.VMEM((seq + 2 * POOL_PAD_ROWS, d_p), F32)],
        compiler_params=pltpu.CompilerParams(
            dimension_semantics=("arbitrary",), vmem_limit_bytes=V7X_VMEM_LIMIT_BYTES),
        name="mix_in",
    )(x3d, gpre, win, mcs, pool_w, pool_scale)


def _mix_out_body(x_ref, dft_ref, pq_ref, yp_ref, wout_ref, gpost_ref, o_ref):
    d_f = pq_ref.shape[-1]
    yf = jnp.dot(dft_ref[...], pq_ref[...], preferred_element_type=F32).astype(BF16)
    y = (jnp.dot(yf, wout_ref[0:d_f, :], preferred_element_type=F32)
         + jnp.dot(yp_ref[...], wout_ref[d_f:, :], preferred_element_type=F32))
    o_ref[...] = y

    def post_norm(i, carry):
        r = pl.multiple_of(i * NORM_ROW_CHUNK, NORM_ROW_CHUNK)
        yy = o_ref[pl.ds(r, NORM_ROW_CHUNK), :]
        x = x_ref[pl.ds(r, NORM_ROW_CHUNK), :]
        o_ref[pl.ds(r, NORM_ROW_CHUNK), :] = x + yy * _rms_scale(yy) * gpost_ref[...]
        return carry

    lax.fori_loop(0, o_ref.shape[0] // NORM_ROW_CHUNK, post_norm, 0)


def _mix_out(x3d, dft, pq, yp, wout, gpost, layer):
    n_b, seq, d = x3d.shape
    d_f = pq.shape[-1]
    d_p = yp.shape[-1]
    tr = MIX_OUT_ROW_TILE
    return pl.pallas_call(
        _mix_out_body,
        out_shape=jax.ShapeDtypeStruct((n_b, seq, d), F32),
        grid=(n_b, seq // tr),
        in_specs=[
            pl.BlockSpec((None, tr, d), lambda b, j: (b, j, 0)),
            pl.BlockSpec((tr, 2 * seq), lambda b, j: (j, 0)),
            pl.BlockSpec((None, 2 * seq, d_f), lambda b, j: (b, 0, 0)),
            pl.BlockSpec((None, tr, d_p), lambda b, j: (b, j, 0)),
            pl.BlockSpec((None, d_f + d_p, d), lambda b, j: (layer, 0, 0)),
            pl.BlockSpec((None, 1, d), lambda b, j: (layer, 0, 0)),
        ],
        out_specs=pl.BlockSpec((None, tr, d), lambda b, j: (b, j, 0)),
        compiler_params=pltpu.CompilerParams(
            dimension_semantics=("arbitrary", "arbitrary"), vmem_limit_bytes=V7X_VMEM_LIMIT_BYTES),
        name="mix_out",
    )(x3d, dft, pq, yp, wout, gpost)


def _dft_cos_sin(n):
    k = np.arange(n, dtype=np.int64)
    ang = 2.0 * np.pi * ((k[:, None] * k[None, :]) % n).astype(np.float64) / n
    return np.cos(ang), np.sin(ang)


@functools.lru_cache(maxsize=None)
def _constants(seq, hd):
    cs, ss = _dft_cos_sin(seq)
    seq_dft = np.concatenate([cs, -ss], axis=1).astype(np.float32)
    cc, sc = _dft_cos_sin(hd)
    ortho = 1.0 / np.sqrt(float(seq) * float(hd))
    chan_cs = (np.stack([cc, sc]) * ortho).astype(np.float32)
    return seq_dft, chan_cs


def kernel(x, ffn1_pre_g, ffn1_w_gate, ffn1_w_up, ffn1_w_down, ffn1_post_g, mix_pre_g, w_in, fourier_w, pool_w, pool_scale, w_out, mix_post_g, ffn2_pre_g, ffn2_w_gate, ffn2_w_up, ffn2_w_down, ffn2_post_g):
    n_b, seq, d = x.shape
    n_layers = w_in.shape[0]
    hd = fourier_w.shape[-1]
    seq_dft_np, chan_cs_np = _constants(seq, hd)
    seq_dft = jnp.asarray(seq_dft_np).astype(BF16)
    mcs = _head_dft(jnp.asarray(chan_cs_np), fourier_w)

    def gain(g):
        return g.reshape(n_layers, 1, g.shape[-1])

    f1 = (gain(ffn1_pre_g), ffn1_w_gate.astype(BF16), ffn1_w_up.astype(BF16),
          ffn1_w_down.astype(BF16), gain(ffn1_post_g))
    f2 = (gain(ffn2_pre_g), ffn2_w_gate.astype(BF16), ffn2_w_up.astype(BF16),
          ffn2_w_down.astype(BF16), gain(ffn2_post_g))
    win_b = w_in.astype(BF16)
    wout_b = w_out.astype(BF16)
    pw_b = pool_w.astype(BF16)
    mix_pre, mix_post, pscale = gain(mix_pre_g), gain(mix_post_g), gain(pool_scale)

    for l in range(n_layers):
        x = _ffn(x.reshape(n_b * seq, d), *f1, l).reshape(n_b, seq, d)
        pq, yp = _mix_in(x, mix_pre, win_b, mcs, pw_b, pscale, l)
        x = _mix_out(x, seq_dft, pq.reshape(n_b, 2 * seq, -1), yp, wout_b, mix_post, l)
        x = _ffn(x.reshape(n_b * seq, d), *f2, l).reshape(n_b, seq, d)
    return x
```

```python
import functools

import numpy as np
import jax
import jax.numpy as jnp
from jax import lax
from jax.experimental import pallas as pl
from jax.experimental.pallas import tpu as pltpu

EPS = 1e-6
N_FOURIER_HEADS = 4
POOL_WINDOWS = (2, 4, 8, 16)

V7X_SUBLANES = 8
V7X_VMEM_LIMIT_BYTES = 58 * 1024 * 1024

FFN_TOKEN_TILE = 512
FFN_HIDDEN_CHUNK = 256
NORM_ROW_CHUNK = 128
PRENORM_ROW_TILE = 1024
MIX_ROW_CHUNK = 512
MIX_OUT_ROW_TILE = 1024
POOL_PAD_ROWS = V7X_SUBLANES

BF16 = jnp.bfloat16
F32 = jnp.float32


def _rms_scale(x):
    return lax.rsqrt(jnp.mean(x * x, axis=-1, keepdims=True) + EPS)


def _residual_and_next_norm(y, x, g_post, g_next):
    x_new = x + y * _rms_scale(y) * g_post
    h_new = None if g_next is None else (x_new * _rms_scale(x_new) * g_next).astype(BF16)
    return x_new, h_new


def _gain_spec(layer, width, n_grid):
    if n_grid == 1:
        return pl.BlockSpec((None, 1, width), lambda i: (layer, 0, 0))
    return pl.BlockSpec((None, 1, width), lambda i, j: (layer, 0, 0))


def _prenorm_body(x_ref, g_ref, h_ref):
    x = x_ref[...]
    h_ref[...] = (x * _rms_scale(x) * g_ref[...]).astype(BF16)


def _prenorm(x2d, gains, layer):
    n_tok, d = x2d.shape
    tr = PRENORM_ROW_TILE
    return pl.pallas_call(
        _prenorm_body,
        out_shape=jax.ShapeDtypeStruct((n_tok, d), BF16),
        grid=(n_tok // tr,),
        in_specs=[pl.BlockSpec((tr, d), lambda i: (i, 0)), _gain_spec(layer, d, 1)],
        out_specs=pl.BlockSpec((tr, d), lambda i: (i, 0)),
        compiler_params=pltpu.CompilerParams(dimension_semantics=("arbitrary",)),
        name="prenorm",
    )(x2d, gains)


def _ffn_body(*refs, n_tiles, emit_h):
    if emit_h:
        (h_ref, x_ref, wg_ref, wu_ref, wd_ref, gpost_ref, gnext_ref,
         xo_ref, ho_ref, g_scr, y_scr) = refs
    else:
        h_ref, x_ref, wg_ref, wu_ref, wd_ref, gpost_ref, xo_ref, g_scr, y_scr = refs
        gnext_ref = ho_ref = None
    tm = h_ref.shape[0]
    d_ff = wg_ref.shape[1]
    i = pl.program_id(0)

    def finish_previous_tile():
        g_post = 0.5 * gpost_ref[...]
        g_next = gnext_ref[...] if emit_h else None
        for c in range(tm // NORM_ROW_CHUNK):
            rows = slice(c * NORM_ROW_CHUNK, (c + 1) * NORM_ROW_CHUNK)
            x_new, h_new = _residual_and_next_norm(y_scr[rows, :], x_ref[rows, :], g_post, g_next)
            xo_ref[rows, :] = x_new
            if emit_h:
                ho_ref[rows, :] = h_new

    def matmuls():
        for c in range(d_ff // FFN_HIDDEN_CHUNK):
            sl = slice(c * FFN_HIDDEN_CHUNK, (c + 1) * FFN_HIDDEN_CHUNK)
            a = jnp.dot(h_ref[...], wg_ref[:, sl], preferred_element_type=F32)
            b = jnp.dot(h_ref[...], wu_ref[:, sl], preferred_element_type=F32)
            g_scr[:, sl] = ((a / (1.0 + jnp.exp(-a))) * b).astype(BF16)
        y_scr[...] = jnp.dot(g_scr[...], wd_ref[...], preferred_element_type=F32)

    @pl.when(i == 0)
    def _():
        y_scr[...] = jnp.zeros(y_scr.shape, F32)

    @pl.when(i < n_tiles)
    def _():
        finish_previous_tile()
        matmuls()

    @pl.when(i == n_tiles)
    def _():
        finish_previous_tile()


def _ffn(h2d, x2d, wg, wu, wd, gpost, gnext, layer, next_layer):
    n_tok, d = x2d.shape
    d_ff = wg.shape[-1]
    tm = FFN_TOKEN_TILE
    n_tiles = n_tok // tm
    emit_h = gnext is not None

    def cur(i):
        return (jnp.minimum(i, n_tiles - 1), 0)

    def prev(i):
        return (jnp.maximum(i - 1, 0), 0)

    in_specs = [
        pl.BlockSpec((tm, d), cur),
        pl.BlockSpec((tm, d), prev),
        pl.BlockSpec((None, d, d_ff), lambda i: (layer, 0, 0)),
        pl.BlockSpec((None, d, d_ff), lambda i: (layer, 0, 0)),
        pl.BlockSpec((None, d_ff, d), lambda i: (layer, 0, 0)),
        _gain_spec(layer, d, 1),
    ]
    args = [h2d, x2d, wg, wu, wd, gpost]
    out_shape = [jax.ShapeDtypeStruct((n_tok, d), F32)]
    out_specs = [pl.BlockSpec((tm, d), prev)]
    if emit_h:
        in_specs.append(_gain_spec(next_layer, d, 1))
        args.append(gnext)
        out_shape.append(jax.ShapeDtypeStruct((n_tok, d), BF16))
        out_specs.append(pl.BlockSpec((tm, d), prev))
    out = pl.pallas_call(
        functools.partial(_ffn_body, n_tiles=n_tiles, emit_h=emit_h),
        out_shape=tuple(out_shape),
        grid=(n_tiles + 1,),
        in_specs=in_specs,
        out_specs=tuple(out_specs),
        scratch_shapes=[pltpu.VMEM((tm, d_ff), BF16), pltpu.VMEM((tm, d), F32)],
        compiler_params=pltpu.CompilerParams(
            dimension_semantics=("arbitrary",), vmem_limit_bytes=V7X_VMEM_LIMIT_BYTES),
        name="ffn",
    )(*args)
    return out if emit_h else out[0]


def _head_dft_body(cs_ref, fw_ref, o_ref):
    hd = fw_ref.shape[0]
    fw = fw_ref[...]
    o_ref[:, :hd] = jnp.dot(cs_ref[0], fw, preferred_element_type=F32,
                            precision=lax.Precision.HIGHEST).astype(BF16)
    o_ref[:, hd:] = jnp.dot(cs_ref[1], fw, preferred_element_type=F32,
                            precision=lax.Precision.HIGHEST).astype(BF16)


def _head_dft(chan_cs, fourier_w):
    n_layers, n_heads, hd, _ = fourier_w.shape
    return pl.pallas_call(
        _head_dft_body,
        out_shape=jax.ShapeDtypeStruct((n_layers, n_heads, hd, 2 * hd), BF16),
        grid=(n_layers, n_heads),
        in_specs=[
            pl.BlockSpec((2, hd, hd), lambda l, h: (0, 0, 0)),
            pl.BlockSpec((None, None, hd, hd), lambda l, h: (l, h, 0, 0)),
        ],
        out_specs=pl.BlockSpec((None, None, hd, 2 * hd), lambda l, h: (l, h, 0, 0)),
        compiler_params=pltpu.CompilerParams(dimension_semantics=("arbitrary", "arbitrary")),
        name="head_dft",
    )(chan_cs, fourier_w)


def _mix_in_body(h_ref, win_ref, mcs_ref, pw_ref, pscale_ref, pq_ref, yp_ref, pad_scr):
    seq, _ = h_ref.shape
    d_f = pq_ref.shape[-1]
    hd = d_f // N_FOURIER_HEADS
    d_p = yp_ref.shape[-1]
    gd = d_p // len(POOL_WINDOWS)
    rc = MIX_ROW_CHUNK
    pad = POOL_PAD_ROWS

    zeros = jnp.zeros((pad, d_p), F32)
    pad_scr[0:pad, :] = zeros
    pad_scr[pad + seq:pad + seq + pad, :] = zeros

    for ci in range(seq // rc):
        r0 = ci * rc
        u = jnp.dot(h_ref[r0:r0 + rc, :], win_ref[...], preferred_element_type=F32)
        pad_scr[pad + r0:pad + r0 + rc, :] = u[:, d_f:]
        for hh in range(N_FOURIER_HEADS):
            uh = u[:, hh * hd:(hh + 1) * hd].astype(BF16)
            pq = jnp.dot(uh, mcs_ref[hh], preferred_element_type=F32)
            pq_ref[0, r0:r0 + rc, hh * hd:(hh + 1) * hd] = pq[:, :hd].astype(BF16)
            pq_ref[1, r0:r0 + rc, hh * hd:(hh + 1) * hd] = pq[:, hd:].astype(BF16)

    for ci in range(seq // rc):
        r0 = ci * rc
        t = r0 + lax.broadcasted_iota(jnp.int32, (rc, gd), 0)
        for g, win in enumerate(POOL_WINDOWS):
            radius = win // 2
            cols = slice(g * gd, (g + 1) * gd)
            acc = pad_scr[pad + r0 - radius:pad + r0 - radius + rc, cols]
            for j in range(-radius + 1, radius + 1):
                acc = acc + pad_scr[pad + r0 + j:pad + r0 + j + rc, cols]
            count = (jnp.minimum(t + radius, seq - 1) - jnp.maximum(t - radius, 0) + 1).astype(F32)
            diff = acc / count - pad_scr[pad + r0:pad + r0 + rc, cols]
            y = jnp.dot(diff.astype(BF16), pw_ref[g], preferred_element_type=F32)
            yp_ref[r0:r0 + rc, cols] = (y * pscale_ref[:, cols]).astype(BF16)


def _mix_in(h3d, win, mcs, pool_w, pool_scale, layer):
    n_b, seq, d = h3d.shape
    d_mix = win.shape[-1]
    n_heads, hd = mcs.shape[1], mcs.shape[2]
    d_f = n_heads * hd
    d_p = d_mix - d_f
    n_groups, gd = pool_w.shape[1], pool_w.shape[2]
    return pl.pallas_call(
        _mix_in_body,
        out_shape=(jax.ShapeDtypeStruct((n_b, 2, seq, d_f), BF16),
                   jax.ShapeDtypeStruct((n_b, seq, d_p), BF16)),
        grid=(n_b,),
        in_specs=[
            pl.BlockSpec((None, seq, d), lambda b: (b, 0, 0)),
            pl.BlockSpec((None, d, d_mix), lambda b: (layer, 0, 0)),
            pl.BlockSpec((None, n_heads, hd, 2 * hd), lambda b: (layer, 0, 0, 0)),
            pl.BlockSpec((None, n_groups, gd, gd), lambda b: (layer, 0, 0, 0)),
            _gain_spec(layer, d_p, 1),
        ],
        out_specs=(pl.BlockSpec((None, 2, seq, d_f), lambda b: (b, 0, 0, 0)),
                   pl.BlockSpec((None, seq, d_p), lambda b: (b, 0, 0))),
        scratch_shapes=[pltpu.VMEM((seq + 2 * POOL_PAD_ROWS, d_p), F32)],
        compiler_params=pltpu.CompilerParams(
            dimension_semantics=("arbitrary",), vmem_limit_bytes=V7X_VMEM_LIMIT_BYTES),
        name="mix_in",
    )(h3d, win, mcs, pool_w, pool_scale)


def _mix_out_body(x_ref, dft_ref, pq_ref, yp_ref, wout_ref, gpost_ref, gnext_ref, xo_ref, ho_ref):
    d_f = pq_ref.shape[-1]
    yf = jnp.dot(dft_ref[...], pq_ref[...], preferred_element_type=F32).astype(BF16)
    xo_ref[...] = (jnp.dot(yf, wout_ref[0:d_f, :], preferred_element_type=F32)
                   + jnp.dot(yp_ref[...], wout_ref[d_f:, :], preferred_element_type=F32))

    def post_norm(i, carry):
        r = pl.multiple_of(i * NORM_ROW_CHUNK, NORM_ROW_CHUNK)
        rows = pl.ds(r, NORM_ROW_CHUNK)
        x_new, h_new = _residual_and_next_norm(xo_ref[rows, :], x_ref[rows, :],
                                               gpost_ref[...], gnext_ref[...])
        xo_ref[rows, :] = x_new
        ho_ref[rows, :] = h_new
        return carry

    lax.fori_loop(0, xo_ref.shape[0] // NORM_ROW_CHUNK, post_norm, 0)


def _mix_out(x3d, dft, pq, yp, wout, gpost, gnext, layer):
    n_b, seq, d = x3d.shape
    d_f = pq.shape[-1]
    d_p = yp.shape[-1]
    tr = MIX_OUT_ROW_TILE
    return pl.pallas_call(
        _mix_out_body,
        out_shape=(jax.ShapeDtypeStruct((n_b, seq, d), F32),
                   jax.ShapeDtypeStruct((n_b, seq, d), BF16)),
        grid=(n_b, seq // tr),
        in_specs=[
            pl.BlockSpec((None, tr, d), lambda b, j: (b, j, 0)),
            pl.BlockSpec((tr, 2 * seq), lambda b, j: (j, 0)),
            pl.BlockSpec((None, 2 * seq, d_f), lambda b, j: (b, 0, 0)),
            pl.BlockSpec((None, tr, d_p), lambda b, j: (b, j, 0)),
            pl.BlockSpec((None, d_f + d_p, d), lambda b, j: (layer, 0, 0)),
            _gain_spec(layer, d, 2),
            _gain_spec(layer, d, 2),
        ],
        out_specs=(pl.BlockSpec((None, tr, d), lambda b, j: (b, j, 0)),
                   pl.BlockSpec((None, tr, d), lambda b, j: (b, j, 0))),
        compiler_params=pltpu.CompilerParams(
            dimension_semantics=("arbitrary", "arbitrary"), vmem_limit_bytes=V7X_VMEM_LIMIT_BYTES),
        name="mix_out",
    )(x3d, dft, pq, yp, wout, gpost, gnext)


def _dft_cos_sin(n):
    k = np.arange(n, dtype=np.int64)
    ang = 2.0 * np.pi * ((k[:, None] * k[None, :]) % n).astype(np.float64) / n
    return np.cos(ang), np.sin(ang)


@functools.lru_cache(maxsize=None)
def _constants(seq, hd):
    cs, ss = _dft_cos_sin(seq)
    seq_dft = np.concatenate([cs, -ss], axis=1).astype(np.float32)
    cc, sc = _dft_cos_sin(hd)
    ortho = 1.0 / np.sqrt(float(seq) * float(hd))
    chan_cs = (np.stack([cc, sc]) * ortho).astype(np.float32)
    return seq_dft, chan_cs


def kernel(x, ffn1_pre_g, ffn1_w_gate, ffn1_w_up, ffn1_w_down, ffn1_post_g, mix_pre_g, w_in, fourier_w, pool_w, pool_scale, w_out, mix_post_g, ffn2_pre_g, ffn2_w_gate, ffn2_w_up, ffn2_w_down, ffn2_post_g):
    n_b, seq, d = x.shape
    n_layers = w_in.shape[0]
    hd = fourier_w.shape[-1]
    seq_dft_np, chan_cs_np = _constants(seq, hd)
    seq_dft = jnp.asarray(seq_dft_np).astype(BF16)
    mcs = _head_dft(jnp.asarray(chan_cs_np), fourier_w)

    def gain(g):
        return g.reshape(n_layers, 1, g.shape[-1])

    f1_pre, f1_post = gain(ffn1_pre_g), gain(ffn1_post_g)
    f2_pre, f2_post = gain(ffn2_pre_g), gain(ffn2_post_g)
    mix_pre, mix_post, pscale = gain(mix_pre_g), gain(mix_post_g), gain(pool_scale)
    w1 = (ffn1_w_gate.astype(BF16), ffn1_w_up.astype(BF16), ffn1_w_down.astype(BF16))
    w2 = (ffn2_w_gate.astype(BF16), ffn2_w_up.astype(BF16), ffn2_w_down.astype(BF16))
    win_b = w_in.astype(BF16)
    wout_b = w_out.astype(BF16)
    pw_b = pool_w.astype(BF16)

    x2 = x.reshape(n_b * seq, d)
    h2 = _prenorm(x2, f1_pre, 0)
    for l in range(n_layers):
        x2, h2 = _ffn(h2, x2, *w1, f1_post, mix_pre, l, l)
        pq, yp = _mix_in(h2.reshape(n_b, seq, d), win_b, mcs, pw_b, pscale, l)
        x3, h3 = _mix_out(x2.reshape(n_b, seq, d), seq_dft, pq.reshape(n_b, 2 * seq, -1), yp,
                          wout_b, mix_post, f2_pre, l)
        x2, h2 = x3.reshape(n_b * seq, d), h3.reshape(n_b * seq, d)
        if l + 1 < n_layers:
            x2, h2 = _ffn(h2, x2, *w2, f2_post, f1_pre, l, l + 1)
        else:
            x2 = _ffn(h2, x2, *w2, f2_post, None, l, None)
    return x2.reshape(n_b, seq, d)
```

```python
import functools

import numpy as np
import jax
import jax.numpy as jnp
from jax import lax
from jax.experimental import pallas as pl
from jax.experimental.pallas import tpu as pltpu

EPS = 1e-6
N_FOURIER_HEADS = 4
POOL_WINDOWS = (2, 4, 8, 16)

V7X_SUBLANES = 8
V7X_VMEM_LIMIT_BYTES = 58 * 1024 * 1024

FFN_TOKEN_TILE = 512
FFN_HIDDEN_CHUNK = 256
NORM_ROW_CHUNK = 128
PRENORM_ROW_TILE = 1024
MIX_ROW_CHUNK = 512
DFT_RADIX = 4
POOL_PAD_ROWS = 2 * V7X_SUBLANES

BF16 = jnp.bfloat16
F32 = jnp.float32


def _rms_scale(x):
    return lax.rsqrt(jnp.mean(x * x, axis=-1, keepdims=True) + EPS)


def _residual_and_next_norm(y, x, g_post, g_next):
    x_new = x + y * _rms_scale(y) * g_post
    h_new = None if g_next is None else (x_new * _rms_scale(x_new) * g_next).astype(BF16)
    return x_new, h_new


def _gain_spec(layer, width, n_grid):
    if n_grid == 1:
        return pl.BlockSpec((None, 1, width), lambda i: (layer, 0, 0))
    return pl.BlockSpec((None, 1, width), lambda i, j: (layer, 0, 0))


def _prenorm_body(x_ref, g_ref, h_ref):
    x = x_ref[...]
    h_ref[...] = (x * _rms_scale(x) * g_ref[...]).astype(BF16)


def _prenorm(x2d, gains, layer):
    n_tok, d = x2d.shape
    tr = PRENORM_ROW_TILE
    return pl.pallas_call(
        _prenorm_body,
        out_shape=jax.ShapeDtypeStruct((n_tok, d), BF16),
        grid=(n_tok // tr,),
        in_specs=[pl.BlockSpec((tr, d), lambda i: (i, 0)), _gain_spec(layer, d, 1)],
        out_specs=pl.BlockSpec((tr, d), lambda i: (i, 0)),
        compiler_params=pltpu.CompilerParams(dimension_semantics=("arbitrary",)),
        name="prenorm",
    )(x2d, gains)


def _ffn_body(*refs, n_tiles, emit_h):
    if emit_h:
        (h_ref, x_ref, wg_ref, wu_ref, wd_ref, gpost_ref, gnext_ref,
         xo_ref, ho_ref, g_scr, y_scr) = refs
    else:
        h_ref, x_ref, wg_ref, wu_ref, wd_ref, gpost_ref, xo_ref, g_scr, y_scr = refs
        gnext_ref = ho_ref = None
    tm = h_ref.shape[0]
    d_ff = wg_ref.shape[1]
    i = pl.program_id(0)

    def finish_previous_tile():
        g_post = 0.5 * gpost_ref[...]
        g_next = gnext_ref[...] if emit_h else None
        for c in range(tm // NORM_ROW_CHUNK):
            rows = slice(c * NORM_ROW_CHUNK, (c + 1) * NORM_ROW_CHUNK)
            x_new, h_new = _residual_and_next_norm(y_scr[rows, :], x_ref[rows, :], g_post, g_next)
            xo_ref[rows, :] = x_new
            if emit_h:
                ho_ref[rows, :] = h_new

    def matmuls():
        for c in range(d_ff // FFN_HIDDEN_CHUNK):
            sl = slice(c * FFN_HIDDEN_CHUNK, (c + 1) * FFN_HIDDEN_CHUNK)
            a = jnp.dot(h_ref[...], wg_ref[:, sl], preferred_element_type=F32)
            b = jnp.dot(h_ref[...], wu_ref[:, sl], preferred_element_type=F32)
            g_scr[:, sl] = ((a / (1.0 + jnp.exp(-a))) * b).astype(BF16)
        y_scr[...] = jnp.dot(g_scr[...], wd_ref[...], preferred_element_type=F32)

    @pl.when(i == 0)
    def _():
        y_scr[...] = jnp.zeros(y_scr.shape, F32)

    @pl.when(i < n_tiles)
    def _():
        finish_previous_tile()
        matmuls()

    @pl.when(i == n_tiles)
    def _():
        finish_previous_tile()


def _ffn(h2d, x2d, wg, wu, wd, gpost, gnext, layer, next_layer):
    n_tok, d = x2d.shape
    d_ff = wg.shape[-1]
    tm = FFN_TOKEN_TILE
    n_tiles = n_tok // tm
    emit_h = gnext is not None

    def cur(i):
        return (jnp.minimum(i, n_tiles - 1), 0)

    def prev(i):
        return (jnp.maximum(i - 1, 0), 0)

    in_specs = [
        pl.BlockSpec((tm, d), cur),
        pl.BlockSpec((tm, d), prev),
        pl.BlockSpec((None, d, d_ff), lambda i: (layer, 0, 0)),
        pl.BlockSpec((None, d, d_ff), lambda i: (layer, 0, 0)),
        pl.BlockSpec((None, d_ff, d), lambda i: (layer, 0, 0)),
        _gain_spec(layer, d, 1),
    ]
    args = [h2d, x2d, wg, wu, wd, gpost]
    out_shape = [jax.ShapeDtypeStruct((n_tok, d), F32)]
    out_specs = [pl.BlockSpec((tm, d), prev)]
    if emit_h:
        in_specs.append(_gain_spec(next_layer, d, 1))
        args.append(gnext)
        out_shape.append(jax.ShapeDtypeStruct((n_tok, d), BF16))
        out_specs.append(pl.BlockSpec((tm, d), prev))
    out = pl.pallas_call(
        functools.partial(_ffn_body, n_tiles=n_tiles, emit_h=emit_h),
        out_shape=tuple(out_shape),
        grid=(n_tiles + 1,),
        in_specs=in_specs,
        out_specs=tuple(out_specs),
        scratch_shapes=[pltpu.VMEM((tm, d_ff), BF16), pltpu.VMEM((tm, d), F32)],
        compiler_params=pltpu.CompilerParams(
            dimension_semantics=("arbitrary",), vmem_limit_bytes=V7X_VMEM_LIMIT_BYTES),
        name="ffn",
    )(*args)
    return out if emit_h else out[0]


def _head_dft_body(cs_ref, fw_ref, o_ref):
    hd = fw_ref.shape[0]
    fw = fw_ref[...]
    o_ref[:, :hd] = jnp.dot(cs_ref[0], fw, preferred_element_type=F32,
                            precision=lax.Precision.HIGHEST).astype(BF16)
    o_ref[:, hd:] = jnp.dot(cs_ref[1], fw, preferred_element_type=F32,
                            precision=lax.Precision.HIGHEST).astype(BF16)


def _head_dft(chan_cs, fourier_w):
    n_layers, n_heads, hd, _ = fourier_w.shape
    return pl.pallas_call(
        _head_dft_body,
        out_shape=jax.ShapeDtypeStruct((n_layers, n_heads, hd, 2 * hd), BF16),
        grid=(n_layers, n_heads),
        in_specs=[
            pl.BlockSpec((2, hd, hd), lambda l, h: (0, 0, 0)),
            pl.BlockSpec((None, None, hd, hd), lambda l, h: (l, h, 0, 0)),
        ],
        out_specs=pl.BlockSpec((None, None, hd, 2 * hd), lambda l, h: (l, h, 0, 0)),
        compiler_params=pltpu.CompilerParams(dimension_semantics=("arbitrary", "arbitrary")),
        name="head_dft",
    )(chan_cs, fourier_w)


def _mix_in_body(h_ref, win_ref, mcs_ref, pw_ref, pscale_ref, pq_ref, yp_ref, pad_scr, lvl_scr):
    seq, _ = h_ref.shape
    d_f = pq_ref.shape[-1]
    hd = d_f // N_FOURIER_HEADS
    d_p = yp_ref.shape[-1]
    gd = d_p // len(POOL_WINDOWS)
    rc = MIX_ROW_CHUNK
    pad = POOL_PAD_ROWS
    lo, hi = pad // 2, pad + seq + pad // 2

    pad_scr[0:pad, :] = jnp.zeros((pad, d_p), F32)
    pad_scr[pad + seq:pad + seq + pad, :] = jnp.zeros((pad, d_p), F32)
    for buf in range(2):
        lvl_scr[buf, hi:hi + pad // 2, :] = jnp.zeros((pad // 2, d_p), F32)

    for ci in range(seq // rc):
        r0 = ci * rc
        u = jnp.dot(h_ref[r0:r0 + rc, :], win_ref[...], preferred_element_type=F32)
        pad_scr[pad + r0:pad + r0 + rc, :] = u[:, d_f:]
        for hh in range(N_FOURIER_HEADS):
            uh = u[:, hh * hd:(hh + 1) * hd].astype(BF16)
            pq = jnp.dot(uh, mcs_ref[hh], preferred_element_type=F32)
            pq_ref[0, r0:r0 + rc, hh * hd:(hh + 1) * hd] = pq[:, :hd].astype(BF16)
            pq_ref[1, r0:r0 + rc, hh * hd:(hh + 1) * hd] = pq[:, hd:].astype(BF16)

    for g, win in enumerate(POOL_WINDOWS):
        radius = win // 2
        cols = slice(g * gd, (g + 1) * gd)
        fwd, width, buf = pad_scr, 1, 0
        while width < radius:
            src = fwd if width == 1 else fwd.at[1 - buf]
            lvl_scr[buf, lo:hi, cols] = src[lo:hi, cols] + src[lo + width:hi + width, cols]
            fwd, width, buf = lvl_scr, 2 * width, 1 - buf
        src = fwd if width == 1 else fwd.at[1 - buf]
        for ci in range(seq // rc):
            p0 = pad + ci * rc
            t = ci * rc + lax.broadcasted_iota(jnp.int32, (rc, gd), 0)
            total = (src[p0 - radius:p0 - radius + rc, cols] + src[p0:p0 + rc, cols]
                     + pad_scr[p0 + radius:p0 + radius + rc, cols])
            count = (jnp.minimum(t + radius, seq - 1) - jnp.maximum(t - radius, 0) + 1).astype(F32)
            diff = total / count - pad_scr[p0:p0 + rc, cols]
            y = jnp.dot(diff.astype(BF16), pw_ref[g], preferred_element_type=F32)
            yp_ref[ci * rc:(ci + 1) * rc, cols] = (y * pscale_ref[:, cols]).astype(BF16)


def _mix_in(h3d, win, mcs, pool_w, pool_scale, layer):
    n_b, seq, d = h3d.shape
    d_mix = win.shape[-1]
    n_heads, hd = mcs.shape[1], mcs.shape[2]
    d_f = n_heads * hd
    d_p = d_mix - d_f
    n_groups, gd = pool_w.shape[1], pool_w.shape[2]
    return pl.pallas_call(
        _mix_in_body,
        out_shape=(jax.ShapeDtypeStruct((n_b, 2, seq, d_f), BF16),
                   jax.ShapeDtypeStruct((n_b, seq, d_p), BF16)),
        grid=(n_b,),
        in_specs=[
            pl.BlockSpec((None, seq, d), lambda b: (b, 0, 0)),
            pl.BlockSpec((None, d, d_mix), lambda b: (layer, 0, 0)),
            pl.BlockSpec((None, n_heads, hd, 2 * hd), lambda b: (layer, 0, 0, 0)),
            pl.BlockSpec((None, n_groups, gd, gd), lambda b: (layer, 0, 0, 0)),
            _gain_spec(layer, d_p, 1),
        ],
        out_specs=(pl.BlockSpec((None, 2, seq, d_f), lambda b: (b, 0, 0, 0)),
                   pl.BlockSpec((None, seq, d_p), lambda b: (b, 0, 0))),
        scratch_shapes=[pltpu.VMEM((seq + 2 * POOL_PAD_ROWS, d_p), F32),
                        pltpu.VMEM((2, seq + 2 * POOL_PAD_ROWS, d_p), F32)],
        compiler_params=pltpu.CompilerParams(
            dimension_semantics=("arbitrary",), vmem_limit_bytes=V7X_VMEM_LIMIT_BYTES),
        name="mix_in",
    )(h3d, win, mcs, pool_w, pool_scale)


def _mix_out_body(pq_ref, dft_ref, yp_ref, x_ref, wout_ref, gpost_ref, gnext_ref,
                  xo_ref, ho_ref, ab_scr, cat_scr, y_scr, *, n_steps):
    nq = dft_ref.shape[1]
    d_f = pq_ref.shape[-1]
    d = wout_ref.shape[-1]
    t = pl.program_id(0)

    def finish_previous_step():
        g_post = gpost_ref[...]
        g_next = gnext_ref[...]
        for k in range(2):
            lanes = slice(k * d, (k + 1) * d)
            for c in range(nq // NORM_ROW_CHUNK):
                rows = slice(c * NORM_ROW_CHUNK, (c + 1) * NORM_ROW_CHUNK)
                yrows = slice(k * nq + c * NORM_ROW_CHUNK, k * nq + (c + 1) * NORM_ROW_CHUNK)
                x_new, h_new = _residual_and_next_norm(y_scr[yrows, :], x_ref[rows, lanes],
                                                       g_post, g_next)
                xo_ref[rows, lanes] = x_new
                ho_ref[rows, lanes] = h_new

    def matmuls():
        sign = jnp.where(t % 2 == 0, 1.0, -1.0).astype(BF16)
        for c in range(nq // NORM_ROW_CHUNK):
            rows = slice(c * NORM_ROW_CHUNK, (c + 1) * NORM_ROW_CHUNK)
            im_rows = slice(nq + c * NORM_ROW_CHUNK, nq + (c + 1) * NORM_ROW_CHUNK)
            p = [pq_ref[0, n * nq + c * NORM_ROW_CHUNK:n * nq + (c + 1) * NORM_ROW_CHUNK, :]
                 for n in range(4)]
            q = [pq_ref[1, n * nq + c * NORM_ROW_CHUNK:n * nq + (c + 1) * NORM_ROW_CHUNK, :]
                 for n in range(4)]
            ab_scr[0, rows, :] = (p[0] + p[2]) + sign * (p[1] + p[3])
            ab_scr[0, im_rows, :] = (q[0] + q[2]) + sign * (q[1] + q[3])
            ab_scr[1, rows, :] = (p[0] - p[2]) - sign * (q[1] - q[3])
            ab_scr[1, im_rows, :] = (q[0] - q[2]) + sign * (p[1] - p[3])
        for k in range(2):
            yf = jnp.dot(dft_ref[k], ab_scr[k], preferred_element_type=F32)
            cat_scr[k, :, 0:d_f] = yf.astype(BF16)
            cat_scr[k, :, d_f:] = yp_ref[:, k * d_f:(k + 1) * d_f]
            y_scr[k * nq:(k + 1) * nq, :] = jnp.dot(cat_scr[k], wout_ref[...],
                                                    preferred_element_type=F32)

    @pl.when(t == 0)
    def _():
        y_scr[...] = jnp.zeros(y_scr.shape, F32)

    @pl.when(t < n_steps)
    def _():
        finish_previous_step()
        matmuls()

    @pl.when(t == n_steps)
    def _():
        finish_previous_step()


def _mix_out(x3d, dft4, pq, yp, wout, gpost, gnext, layer):
    n_b, seq, d = x3d.shape
    d_f = pq.shape[-1]
    d_p = yp.shape[-1]
    assert d_f == d_p, "the pooled and Fourier halves share one lane-slab width"
    nq = seq // DFT_RADIX
    n_steps = 2 * n_b
    xv = x3d.reshape(n_b, nq, DFT_RADIX * d)
    ypv = yp.reshape(n_b, nq, DFT_RADIX * d_p)

    def cur(t):
        s = jnp.minimum(t, n_steps - 1)
        return s // 2, s % 2

    def prev(t):
        s = jnp.maximum(t - 1, 0)
        return s // 2, s % 2

    def x_map(t):
        b, jj = prev(t)
        return (b, 0, jj)

    xo, ho = pl.pallas_call(
        functools.partial(_mix_out_body, n_steps=n_steps),
        out_shape=(jax.ShapeDtypeStruct(xv.shape, F32), jax.ShapeDtypeStruct(xv.shape, BF16)),
        grid=(n_steps + 1,),
        in_specs=[
            pl.BlockSpec((None, 2, seq, d_f), lambda t: (cur(t)[0], 0, 0, 0)),
            pl.BlockSpec((2, nq, 2 * nq), lambda t: (cur(t)[1], 0, 0)),
            pl.BlockSpec((None, nq, 2 * d_p), lambda t: (cur(t)[0], 0, cur(t)[1])),
            pl.BlockSpec((None, nq, 2 * d), x_map),
            pl.BlockSpec((None, d_f + d_p, d), lambda t: (layer, 0, 0)),
            _gain_spec(layer, d, 1),
            _gain_spec(layer, d, 1),
        ],
        out_specs=(pl.BlockSpec((None, nq, 2 * d), x_map), pl.BlockSpec((None, nq, 2 * d), x_map)),
        scratch_shapes=[pltpu.VMEM((2, 2 * nq, d_f), BF16), pltpu.VMEM((2, nq, d_f + d_p), BF16),
                        pltpu.VMEM((2 * nq, d), F32)],
        compiler_params=pltpu.CompilerParams(
            dimension_semantics=("arbitrary",), vmem_limit_bytes=V7X_VMEM_LIMIT_BYTES),
        name="mix_out",
    )(pq, dft4, ypv, xv, wout, gpost, gnext)
    return xo.reshape(n_b, seq, d), ho.reshape(n_b, seq, d)


def _dft_cos_sin(n):
    k = np.arange(n, dtype=np.int64)
    ang = 2.0 * np.pi * ((k[:, None] * k[None, :]) % n).astype(np.float64) / n
    return np.cos(ang), np.sin(ang)


@functools.lru_cache(maxsize=None)
def _constants(seq, hd):
    nq = seq // DFT_RADIX
    n2 = np.arange(nq, dtype=np.int64)[None, None, :]
    k = (DFT_RADIX * np.arange(nq, dtype=np.int64)[None, :, None]
         + np.arange(DFT_RADIX, dtype=np.int64)[:, None, None])
    ang = 2.0 * np.pi * ((n2 * k) % seq).astype(np.float64) / seq
    seq_dft = np.concatenate([np.cos(ang), -np.sin(ang)], axis=2).astype(np.float32)
    cc, sc = _dft_cos_sin(hd)
    ortho = 1.0 / np.sqrt(float(seq) * float(hd))
    chan_cs = (np.stack([cc, sc]) * ortho).astype(np.float32)
    return seq_dft, chan_cs


def kernel(x, ffn1_pre_g, ffn1_w_gate, ffn1_w_up, ffn1_w_down, ffn1_post_g, mix_pre_g, w_in, fourier_w, pool_w, pool_scale, w_out, mix_post_g, ffn2_pre_g, ffn2_w_gate, ffn2_w_up, ffn2_w_down, ffn2_post_g):
    n_b, seq, d = x.shape
    n_layers = w_in.shape[0]
    hd = fourier_w.shape[-1]
    seq_dft_np, chan_cs_np = _constants(seq, hd)
    seq_dft = jnp.asarray(seq_dft_np).astype(BF16)
    mcs = _head_dft(jnp.asarray(chan_cs_np), fourier_w)

    def gain(g):
        return g.reshape(n_layers, 1, g.shape[-1])

    f1_pre, f1_post = gain(ffn1_pre_g), gain(ffn1_post_g)
    f2_pre, f2_post = gain(ffn2_pre_g), gain(ffn2_post_g)
    mix_pre, mix_post, pscale = gain(mix_pre_g), gain(mix_post_g), gain(pool_scale)
    w1 = (ffn1_w_gate.astype(BF16), ffn1_w_up.astype(BF16), ffn1_w_down.astype(BF16))
    w2 = (ffn2_w_gate.astype(BF16), ffn2_w_up.astype(BF16), ffn2_w_down.astype(BF16))
    win_b = w_in.astype(BF16)
    wout_b = w_out.astype(BF16)
    pw_b = pool_w.astype(BF16)

    x2 = x.reshape(n_b * seq, d)
    h2 = _prenorm(x2, f1_pre, 0)
    for l in range(n_layers):
        x2, h2 = _ffn(h2, x2, *w1, f1_post, mix_pre, l, l)
        pq, yp = _mix_in(h2.reshape(n_b, seq, d), win_b, mcs, pw_b, pscale, l)
        x3, h3 = _mix_out(x2.reshape(n_b, seq, d), seq_dft, pq, yp, wout_b, mix_post, f2_pre, l)
        x2, h2 = x3.reshape(n_b * seq, d), h3.reshape(n_b * seq, d)
        if l + 1 < n_layers:
            x2, h2 = _ffn(h2, x2, *w2, f2_post, f1_pre, l, l + 1)
        else:
            x2 = _ffn(h2, x2, *w2, f2_post, None, l, None)
    return x2.reshape(n_b, seq, d)
```

```python
import functools

import numpy as np
import jax
import jax.numpy as jnp
from jax import lax
from jax.experimental import pallas as pl
from jax.experimental.pallas import tpu as pltpu

EPS = 1e-6
N_FOURIER_HEADS = 4
POOL_WINDOWS = (2, 4, 8, 16)

V7X_SUBLANES = 8
V7X_LANES = 128
V7X_VMEM_LIMIT_BYTES = 58 * 1024 * 1024

FFN_TOKEN_TILE = 512
FFN_HIDDEN_CHUNK = 256
NORM_ROW_CHUNK = 128
PRENORM_ROW_TILE = 1024
MIX_ROW_CHUNK = 512
DFT_RADIX = 2
MIX_OUT_ROW_TILE = 1024
POOL_PAD_ROWS = 2 * V7X_SUBLANES

BF16 = jnp.bfloat16
F32 = jnp.float32


def _rms_scale(x):
    return lax.rsqrt(jnp.mean(x * x, axis=-1, keepdims=True) + EPS)


def _residual_and_next_norm(y, x, g_post, g_next):
    x_new = x + y * _rms_scale(y) * g_post
    h_new = None if g_next is None else (x_new * _rms_scale(x_new) * g_next).astype(BF16)
    return x_new, h_new


def _gain_spec(layer, width, n_grid):
    if n_grid == 1:
        return pl.BlockSpec((None, 1, width), lambda i: (layer, 0, 0))
    return pl.BlockSpec((None, 1, width), lambda i, j: (layer, 0, 0))


def _prenorm_body(x_ref, g_ref, h_ref):
    x = x_ref[...]
    h_ref[...] = (x * _rms_scale(x) * g_ref[...]).astype(BF16)


def _prenorm(x2d, gains, layer):
    n_tok, d = x2d.shape
    tr = PRENORM_ROW_TILE
    return pl.pallas_call(
        _prenorm_body,
        out_shape=jax.ShapeDtypeStruct((n_tok, d), BF16),
        grid=(n_tok // tr,),
        in_specs=[pl.BlockSpec((tr, d), lambda i: (i, 0)), _gain_spec(layer, d, 1)],
        out_specs=pl.BlockSpec((tr, d), lambda i: (i, 0)),
        compiler_params=pltpu.CompilerParams(dimension_semantics=("arbitrary",)),
        name="prenorm",
    )(x2d, gains)


def _ffn_body(*refs, n_tiles, emit_h):
    if emit_h:
        (h_ref, x_ref, wg_ref, wu_ref, wd_ref, gpost_ref, gnext_ref,
         xo_ref, ho_ref, g_scr, y_scr) = refs
    else:
        h_ref, x_ref, wg_ref, wu_ref, wd_ref, gpost_ref, xo_ref, g_scr, y_scr = refs
        gnext_ref = ho_ref = None
    tm = h_ref.shape[0]
    d_ff = wg_ref.shape[1]
    i = pl.program_id(0)

    def finish_previous_tile():
        g_post = 0.5 * gpost_ref[...]
        g_next = gnext_ref[...] if emit_h else None
        for c in range(tm // NORM_ROW_CHUNK):
            rows = slice(c * NORM_ROW_CHUNK, (c + 1) * NORM_ROW_CHUNK)
            x_new, h_new = _residual_and_next_norm(y_scr[rows, :], x_ref[rows, :], g_post, g_next)
            xo_ref[rows, :] = x_new
            if emit_h:
                ho_ref[rows, :] = h_new

    def matmuls():
        for c in range(d_ff // FFN_HIDDEN_CHUNK):
            sl = slice(c * FFN_HIDDEN_CHUNK, (c + 1) * FFN_HIDDEN_CHUNK)
            a = jnp.dot(h_ref[...], wg_ref[:, sl], preferred_element_type=F32)
            b = jnp.dot(h_ref[...], wu_ref[:, sl], preferred_element_type=F32)
            g_scr[:, sl] = ((a / (1.0 + jnp.exp(-a))) * b).astype(BF16)
        y_scr[...] = jnp.dot(g_scr[...], wd_ref[...], preferred_element_type=F32)

    @pl.when(i == 0)
    def _():
        y_scr[...] = jnp.zeros(y_scr.shape, F32)

    @pl.when(i < n_tiles)
    def _():
        finish_previous_tile()
        matmuls()

    @pl.when(i == n_tiles)
    def _():
        finish_previous_tile()


def _ffn(h2d, x2d, wg, wu, wd, gpost, gnext, layer, next_layer):
    n_tok, d = x2d.shape
    d_ff = wg.shape[-1]
    tm = FFN_TOKEN_TILE
    n_tiles = n_tok // tm
    emit_h = gnext is not None

    def cur(i):
        return (jnp.minimum(i, n_tiles - 1), 0)

    def prev(i):
        return (jnp.maximum(i - 1, 0), 0)

    in_specs = [
        pl.BlockSpec((tm, d), cur),
        pl.BlockSpec((tm, d), prev),
        pl.BlockSpec((None, d, d_ff), lambda i: (layer, 0, 0)),
        pl.BlockSpec((None, d, d_ff), lambda i: (layer, 0, 0)),
        pl.BlockSpec((None, d_ff, d), lambda i: (layer, 0, 0)),
        _gain_spec(layer, d, 1),
    ]
    args = [h2d, x2d, wg, wu, wd, gpost]
    out_shape = [jax.ShapeDtypeStruct((n_tok, d), F32)]
    out_specs = [pl.BlockSpec((tm, d), prev)]
    if emit_h:
        in_specs.append(_gain_spec(next_layer, d, 1))
        args.append(gnext)
        out_shape.append(jax.ShapeDtypeStruct((n_tok, d), BF16))
        out_specs.append(pl.BlockSpec((tm, d), prev))
    out = pl.pallas_call(
        functools.partial(_ffn_body, n_tiles=n_tiles, emit_h=emit_h),
        out_shape=tuple(out_shape),
        grid=(n_tiles + 1,),
        in_specs=in_specs,
        out_specs=tuple(out_specs),
        scratch_shapes=[pltpu.VMEM((tm, d_ff), BF16), pltpu.VMEM((tm, d), F32)],
        compiler_params=pltpu.CompilerParams(
            dimension_semantics=("arbitrary",), vmem_limit_bytes=V7X_VMEM_LIMIT_BYTES),
        name="ffn",
    )(*args)
    return out if emit_h else out[0]


def _head_dft_body(cs_ref, fw_ref, o_ref):
    hd = fw_ref.shape[0]
    fw = fw_ref[...]
    o_ref[:, :hd] = jnp.dot(cs_ref[0], fw, preferred_element_type=F32,
                            precision=lax.Precision.HIGHEST).astype(BF16)
    o_ref[:, hd:] = jnp.dot(cs_ref[1], fw, preferred_element_type=F32,
                            precision=lax.Precision.HIGHEST).astype(BF16)


def _head_dft(chan_cs, fourier_w):
    n_layers, n_heads, hd, _ = fourier_w.shape
    return pl.pallas_call(
        _head_dft_body,
        out_shape=jax.ShapeDtypeStruct((n_layers, n_heads, hd, 2 * hd), BF16),
        grid=(n_layers, n_heads),
        in_specs=[
            pl.BlockSpec((2, hd, hd), lambda l, h: (0, 0, 0)),
            pl.BlockSpec((None, None, hd, hd), lambda l, h: (l, h, 0, 0)),
        ],
        out_specs=pl.BlockSpec((None, None, hd, 2 * hd), lambda l, h: (l, h, 0, 0)),
        compiler_params=pltpu.CompilerParams(dimension_semantics=("arbitrary", "arbitrary")),
        name="head_dft",
    )(chan_cs, fourier_w)


def _mix_in_body(h_ref, win_ref, mcs_ref, pw_ref, pscale_ref, pq_ref, yp_ref, pad_scr, lvl_scr):
    seq, _ = h_ref.shape
    d_f = pq_ref.shape[-1]
    hd = d_f // N_FOURIER_HEADS
    d_p = yp_ref.shape[-1]
    gd = d_p // len(POOL_WINDOWS)
    rc = MIX_ROW_CHUNK
    pad = POOL_PAD_ROWS
    lo, hi = pad // 2, pad + seq + pad // 2

    pad_scr[0:pad, :] = jnp.zeros((pad, d_p), F32)
    pad_scr[pad + seq:pad + seq + pad, :] = jnp.zeros((pad, d_p), F32)
    for buf in range(2):
        lvl_scr[buf, hi:hi + pad // 2, :] = jnp.zeros((pad // 2, d_p), F32)

    half = seq // 2
    for ci in range(half // rc):
        r0 = ci * rc
        pq_halves = []
        for base in (r0, half + r0):
            u = jnp.dot(h_ref[base:base + rc, :], win_ref[...], preferred_element_type=F32)
            pad_scr[pad + base:pad + base + rc, :] = u[:, d_f:]
            pq_halves.append([
                jnp.dot(u[:, hh * hd:(hh + 1) * hd].astype(BF16), mcs_ref[hh],
                        preferred_element_type=F32)
                for hh in range(N_FOURIER_HEADS)])
        for hh in range(N_FOURIER_HEADS):
            cols = slice(hh * hd, (hh + 1) * hd)
            lo_pq, hi_pq = pq_halves[0][hh], pq_halves[1][hh]
            for c, a in enumerate((lo_pq + hi_pq, lo_pq - hi_pq)):
                pq_ref[c, r0:r0 + rc, cols] = a[:, :hd].astype(BF16)
                pq_ref[c, half + r0:half + r0 + rc, cols] = a[:, hd:].astype(BF16)

    for g, win in enumerate(POOL_WINDOWS):
        radius = win // 2
        cols = slice(g * gd, (g + 1) * gd)
        fwd, width, buf = pad_scr, 1, 0
        while width < radius:
            src = fwd if width == 1 else fwd.at[1 - buf]
            lvl_scr[buf, lo:hi, cols] = src[lo:hi, cols] + src[lo + width:hi + width, cols]
            fwd, width, buf = lvl_scr, 2 * width, 1 - buf
        src = fwd if width == 1 else fwd.at[1 - buf]
        for ci in range(seq // rc):
            p0 = pad + ci * rc
            t = ci * rc + lax.broadcasted_iota(jnp.int32, (rc, gd), 0)
            total = (src[p0 - radius:p0 - radius + rc, cols] + src[p0:p0 + rc, cols]
                     + pad_scr[p0 + radius:p0 + radius + rc, cols])
            count = (jnp.minimum(t + radius, seq - 1) - jnp.maximum(t - radius, 0) + 1).astype(F32)
            diff = total / count - pad_scr[p0:p0 + rc, cols]
            y = jnp.dot(diff.astype(BF16), pw_ref[g], preferred_element_type=F32)
            yp_ref[ci * rc:(ci + 1) * rc, cols] = (y * pscale_ref[:, cols]).astype(BF16)


def _mix_in(h3d, win, mcs, pool_w, pool_scale, layer):
    n_b, seq, d = h3d.shape
    d_mix = win.shape[-1]
    n_heads, hd = mcs.shape[1], mcs.shape[2]
    d_f = n_heads * hd
    d_p = d_mix - d_f
    n_groups, gd = pool_w.shape[1], pool_w.shape[2]
    return pl.pallas_call(
        _mix_in_body,
        out_shape=(jax.ShapeDtypeStruct((n_b, 2, seq, d_f), BF16),
                   jax.ShapeDtypeStruct((n_b, seq, d_p), BF16)),
        grid=(n_b,),
        in_specs=[
            pl.BlockSpec((None, seq, d), lambda b: (b, 0, 0)),
            pl.BlockSpec((None, d, d_mix), lambda b: (layer, 0, 0)),
            pl.BlockSpec((None, n_heads, hd, 2 * hd), lambda b: (layer, 0, 0, 0)),
            pl.BlockSpec((None, n_groups, gd, gd), lambda b: (layer, 0, 0, 0)),
            _gain_spec(layer, d_p, 1),
        ],
        out_specs=(pl.BlockSpec((None, 2, seq, d_f), lambda b: (b, 0, 0, 0)),
                   pl.BlockSpec((None, seq, d_p), lambda b: (b, 0, 0))),
        scratch_shapes=[pltpu.VMEM((seq + 2 * POOL_PAD_ROWS, d_p), F32),
                        pltpu.VMEM((2, seq + 2 * POOL_PAD_ROWS, d_p), F32)],
        compiler_params=pltpu.CompilerParams(
            dimension_semantics=("arbitrary",), vmem_limit_bytes=V7X_VMEM_LIMIT_BYTES),
        name="mix_in",
    )(h3d, win, mcs, pool_w, pool_scale)


def _mix_out_body(ab_ref, dft_ref, yp_ref, x_ref, wout_ref, gpost_ref, gnext_ref,
                  xo_ref, ho_ref, il_scr, cat_scr, y_scr, *, n_steps):
    tr, d = x_ref.shape
    d_f = ab_ref.shape[-1]
    n_slabs = il_scr.shape[0]
    slab = il_scr.shape[-1]
    t = pl.program_id(0)

    def finish_previous_step():
        g_post = gpost_ref[...]
        g_next = gnext_ref[...]
        for c in range(tr // NORM_ROW_CHUNK):
            rows = slice(c * NORM_ROW_CHUNK, (c + 1) * NORM_ROW_CHUNK)
            x_new, h_new = _residual_and_next_norm(y_scr[rows, :], x_ref[rows, :], g_post, g_next)
            xo_ref[rows, :] = x_new
            ho_ref[rows, :] = h_new

    def matmuls():
        j0 = pl.multiple_of((t % 2) * (tr // 2), tr // 2)
        for c in range(2):
            yf = jnp.dot(dft_ref[c, pl.ds(j0, tr // 2), :], ab_ref[c],
                         preferred_element_type=F32)
            for s in range(n_slabs):
                il_scr[s, pl.ds(c, tr // 2, stride=2), :] = yf[:, s * slab:(s + 1) * slab]
        for s in range(n_slabs):
            cat_scr[:, s * slab:(s + 1) * slab] = il_scr[s].astype(BF16)
        cat_scr[:, d_f:] = yp_ref[...]
        y_scr[...] = jnp.dot(cat_scr[...], wout_ref[...], preferred_element_type=F32)

    @pl.when(t == 0)
    def _():
        y_scr[...] = jnp.zeros(y_scr.shape, F32)

    @pl.when(t < n_steps)
    def _():
        finish_previous_step()
        matmuls()

    @pl.when(t == n_steps)
    def _():
        finish_previous_step()


def _mix_out(x3d, dft2, ab, yp, wout, gpost, gnext, layer):
    n_b, seq, d = x3d.shape
    d_f = ab.shape[-1]
    d_p = yp.shape[-1]
    tr = MIX_OUT_ROW_TILE
    tiles_per_seq = seq // tr
    assert tiles_per_seq == 2, "the step parity selects the half of the class rows"
    n_steps = n_b * tiles_per_seq

    def cur(t):
        s = jnp.minimum(t, n_steps - 1)
        return s // tiles_per_seq, s % tiles_per_seq

    def prev(t):
        s = jnp.maximum(t - 1, 0)
        return s // tiles_per_seq, s % tiles_per_seq

    def x_map(t):
        b, jh = prev(t)
        return (b, jh, 0)

    return pl.pallas_call(
        functools.partial(_mix_out_body, n_steps=n_steps),
        out_shape=(jax.ShapeDtypeStruct((n_b, seq, d), F32),
                   jax.ShapeDtypeStruct((n_b, seq, d), BF16)),
        grid=(n_steps + 1,),
        in_specs=[
            pl.BlockSpec((None, 2, seq, d_f), lambda t: (cur(t)[0], 0, 0, 0)),
            pl.BlockSpec((2, seq // 2, seq), lambda t: (0, 0, 0)),
            pl.BlockSpec((None, tr, d_p), lambda t: (cur(t)[0], cur(t)[1], 0)),
            pl.BlockSpec((None, tr, d), x_map),
            pl.BlockSpec((None, d_f + d_p, d), lambda t: (layer, 0, 0)),
            _gain_spec(layer, d, 1),
            _gain_spec(layer, d, 1),
        ],
        out_specs=(pl.BlockSpec((None, tr, d), x_map), pl.BlockSpec((None, tr, d), x_map)),
        scratch_shapes=[pltpu.VMEM((d_f // V7X_LANES, tr, V7X_LANES), F32),
                        pltpu.VMEM((tr, d_f + d_p), BF16),
                        pltpu.VMEM((tr, d), F32)],
        compiler_params=pltpu.CompilerParams(
            dimension_semantics=("arbitrary",), vmem_limit_bytes=V7X_VMEM_LIMIT_BYTES),
        name="mix_out",
    )(ab, dft2, yp, x3d, wout, gpost, gnext)


def _dft_cos_sin(n):
    k = np.arange(n, dtype=np.int64)
    ang = 2.0 * np.pi * ((k[:, None] * k[None, :]) % n).astype(np.float64) / n
    return np.cos(ang), np.sin(ang)


@functools.lru_cache(maxsize=None)
def _constants(seq, hd):
    nq = seq // DFT_RADIX
    n2 = np.arange(nq, dtype=np.int64)[None, None, :]
    k = (DFT_RADIX * np.arange(nq, dtype=np.int64)[None, :, None]
         + np.arange(DFT_RADIX, dtype=np.int64)[:, None, None])
    ang = 2.0 * np.pi * ((n2 * k) % seq).astype(np.float64) / seq
    seq_dft = np.concatenate([np.cos(ang), -np.sin(ang)], axis=2).astype(np.float32)
    cc, sc = _dft_cos_sin(hd)
    ortho = 1.0 / np.sqrt(float(seq) * float(hd))
    chan_cs = (np.stack([cc, sc]) * ortho).astype(np.float32)
    return seq_dft, chan_cs


def kernel(x, ffn1_pre_g, ffn1_w_gate, ffn1_w_up, ffn1_w_down, ffn1_post_g, mix_pre_g, w_in, fourier_w, pool_w, pool_scale, w_out, mix_post_g, ffn2_pre_g, ffn2_w_gate, ffn2_w_up, ffn2_w_down, ffn2_post_g):
    n_b, seq, d = x.shape
    n_layers = w_in.shape[0]
    hd = fourier_w.shape[-1]
    seq_dft_np, chan_cs_np = _constants(seq, hd)
    seq_dft = jnp.asarray(seq_dft_np).astype(BF16)
    mcs = _head_dft(jnp.asarray(chan_cs_np), fourier_w)

    def gain(g):
        return g.reshape(n_layers, 1, g.shape[-1])

    f1_pre, f1_post = gain(ffn1_pre_g), gain(ffn1_post_g)
    f2_pre, f2_post = gain(ffn2_pre_g), gain(ffn2_post_g)
    mix_pre, mix_post, pscale = gain(mix_pre_g), gain(mix_post_g), gain(pool_scale)
    w1 = (ffn1_w_gate.astype(BF16), ffn1_w_up.astype(BF16), ffn1_w_down.astype(BF16))
    w2 = (ffn2_w_gate.astype(BF16), ffn2_w_up.astype(BF16), ffn2_w_down.astype(BF16))
    win_b = w_in.astype(BF16)
    wout_b = w_out.astype(BF16)
    pw_b = pool_w.astype(BF16)

    x2 = x.reshape(n_b * seq, d)
    h2 = _prenorm(x2, f1_pre, 0)
    for l in range(n_layers):
        x2, h2 = _ffn(h2, x2, *w1, f1_post, mix_pre, l, l)
        pq, yp = _mix_in(h2.reshape(n_b, seq, d), win_b, mcs, pw_b, pscale, l)
        x3, h3 = _mix_out(x2.reshape(n_b, seq, d), seq_dft, pq, yp, wout_b, mix_post, f2_pre, l)
        x2, h2 = x3.reshape(n_b * seq, d), h3.reshape(n_b * seq, d)
        if l + 1 < n_layers:
            x2, h2 = _ffn(h2, x2, *w2, f2_post, f1_pre, l, l + 1)
        else:
            x2 = _ffn(h2, x2, *w2, f2_post, None, l, None)
    return x2.reshape(n_b, seq, d)
```

```python
import functools

import numpy as np
import jax
import jax.numpy as jnp
from jax import lax
from jax.experimental import pallas as pl
from jax.experimental.pallas import tpu as pltpu

EPS = 1e-6
N_FOURIER_HEADS = 4
POOL_WINDOWS = (2, 4, 8, 16)

V7X_SUBLANES = 8
V7X_LANES = 128
BF16_TILE_ROWS = 2 * V7X_SUBLANES
V7X_VMEM_LIMIT_BYTES = 58 * 1024 * 1024

FFN_TOKEN_TILE = 512
FFN_HIDDEN_CHUNK = 256
NORM_ROW_CHUNK = 128
PRENORM_ROW_TILE = 1024
MIX_ROW_CHUNK = 512
DFT_RADIX = 2
MIX_OUT_ROW_TILE = 1024
POOL_PAD_ROWS = 2 * V7X_SUBLANES

BF16 = jnp.bfloat16
F32 = jnp.float32


def _rms_scale(x):
    return lax.rsqrt(jnp.mean(x * x, axis=-1, keepdims=True) + EPS)


def _residual_and_next_norm(y, x, g_post, g_next):
    x_new = x + y * _rms_scale(y) * g_post
    h_new = None if g_next is None else (x_new * _rms_scale(x_new) * g_next).astype(BF16)
    return x_new, h_new


def _exact_zero_from(v):
    bits = pltpu.bitcast(v, jnp.uint32)
    half_word = jnp.uint32(16)
    zero = lax.shift_right_logical(lax.shift_right_logical(bits, half_word), half_word)
    return pltpu.bitcast(zero, F32)


def _gain_spec(layer, width, n_grid):
    if n_grid == 1:
        return pl.BlockSpec((None, 1, width), lambda i: (layer, 0, 0))
    return pl.BlockSpec((None, 1, width), lambda i, j: (layer, 0, 0))


def _prenorm_body(x_ref, g_ref, h_ref):
    x = x_ref[...]
    h_ref[...] = (x * _rms_scale(x) * g_ref[...]).astype(BF16)


def _prenorm(x2d, gains, layer):
    n_tok, d = x2d.shape
    tr = PRENORM_ROW_TILE
    return pl.pallas_call(
        _prenorm_body,
        out_shape=jax.ShapeDtypeStruct((n_tok, d), BF16),
        grid=(n_tok // tr,),
        in_specs=[pl.BlockSpec((tr, d), lambda i: (i, 0)), _gain_spec(layer, d, 1)],
        out_specs=pl.BlockSpec((tr, d), lambda i: (i, 0)),
        compiler_params=pltpu.CompilerParams(dimension_semantics=("arbitrary",)),
        name="prenorm",
    )(x2d, gains)


def _ffn_body(*refs, n_tiles, emit_h):
    if emit_h:
        (h_ref, x_ref, wg_ref, wu_ref, wd_ref, gpost_ref, gnext_ref,
         xo_ref, ho_ref, g_scr, y_scr) = refs
    else:
        h_ref, x_ref, wg_ref, wu_ref, wd_ref, gpost_ref, xo_ref, g_scr, y_scr = refs
        gnext_ref = ho_ref = None
    tm = h_ref.shape[0]
    d_ff = wg_ref.shape[1]
    i = pl.program_id(0)

    def finish_previous_tile():
        g_post = 0.5 * gpost_ref[...]
        g_next = gnext_ref[...] if emit_h else None
        done = []
        for c in range(tm // NORM_ROW_CHUNK):
            rows = slice(c * NORM_ROW_CHUNK, (c + 1) * NORM_ROW_CHUNK)
            x_new, h_new = _residual_and_next_norm(y_scr[rows, :], x_ref[rows, :], g_post, g_next)
            xo_ref[rows, :] = x_new
            if emit_h:
                ho_ref[rows, :] = h_new
            done.append(x_new[0:BF16_TILE_ROWS, 0:V7X_LANES])
        return done

    def matmuls(done):
        tie_every = max((d_ff // FFN_HIDDEN_CHUNK) // len(done), 1)
        for c in range(d_ff // FFN_HIDDEN_CHUNK):
            sl = slice(c * FFN_HIDDEN_CHUNK, (c + 1) * FFN_HIDDEN_CHUNK)
            a = jnp.dot(h_ref[...], wg_ref[:, sl], preferred_element_type=F32)
            b = jnp.dot(h_ref[...], wu_ref[:, sl], preferred_element_type=F32)
            g_scr[:, sl] = ((a / (1.0 + jnp.exp(-a))) * b).astype(BF16)
            if c % tie_every == tie_every - 1 and c // tie_every < len(done):
                corner = (slice(0, BF16_TILE_ROWS),
                          slice(c * FFN_HIDDEN_CHUNK, c * FFN_HIDDEN_CHUNK + V7X_LANES))
                g_scr[corner] = g_scr[corner] + _exact_zero_from(done[c // tie_every]).astype(BF16)
        y_scr[...] = jnp.dot(g_scr[...], wd_ref[...], preferred_element_type=F32)

    @pl.when(i == 0)
    def _():
        y_scr[...] = jnp.zeros(y_scr.shape, F32)

    @pl.when(i < n_tiles)
    def _():
        matmuls(finish_previous_tile())

    @pl.when(i == n_tiles)
    def _():
        finish_previous_tile()


def _ffn(h2d, x2d, wg, wu, wd, gpost, gnext, layer, next_layer):
    n_tok, d = x2d.shape
    d_ff = wg.shape[-1]
    tm = FFN_TOKEN_TILE
    n_tiles = n_tok // tm
    emit_h = gnext is not None

    def cur(i):
        return (jnp.minimum(i, n_tiles - 1), 0)

    def prev(i):
        return (jnp.maximum(i - 1, 0), 0)

    in_specs = [
        pl.BlockSpec((tm, d), cur),
        pl.BlockSpec((tm, d), prev),
        pl.BlockSpec((None, d, d_ff), lambda i: (layer, 0, 0)),
        pl.BlockSpec((None, d, d_ff), lambda i: (layer, 0, 0)),
        pl.BlockSpec((None, d_ff, d), lambda i: (layer, 0, 0)),
        _gain_spec(layer, d, 1),
    ]
    args = [h2d, x2d, wg, wu, wd, gpost]
    out_shape = [jax.ShapeDtypeStruct((n_tok, d), F32)]
    out_specs = [pl.BlockSpec((tm, d), prev)]
    if emit_h:
        in_specs.append(_gain_spec(next_layer, d, 1))
        args.append(gnext)
        out_shape.append(jax.ShapeDtypeStruct((n_tok, d), BF16))
        out_specs.append(pl.BlockSpec((tm, d), prev))
    out = pl.pallas_call(
        functools.partial(_ffn_body, n_tiles=n_tiles, emit_h=emit_h),
        out_shape=tuple(out_shape),
        grid=(n_tiles + 1,),
        in_specs=in_specs,
        out_specs=tuple(out_specs),
        scratch_shapes=[pltpu.VMEM((tm, d_ff), BF16), pltpu.VMEM((tm, d), F32)],
        compiler_params=pltpu.CompilerParams(
            dimension_semantics=("arbitrary",), vmem_limit_bytes=V7X_VMEM_LIMIT_BYTES),
        name="ffn",
    )(*args)
    return out if emit_h else out[0]


def _head_dft_body(cs_ref, fw_ref, o_ref):
    hd = fw_ref.shape[0]
    fw = fw_ref[...]
    o_ref[:, :hd] = jnp.dot(cs_ref[0], fw, preferred_element_type=F32,
                            precision=lax.Precision.HIGHEST).astype(BF16)
    o_ref[:, hd:] = jnp.dot(cs_ref[1], fw, preferred_element_type=F32,
                            precision=lax.Precision.HIGHEST).astype(BF16)


def _head_dft(chan_cs, fourier_w):
    n_layers, n_heads, hd, _ = fourier_w.shape
    return pl.pallas_call(
        _head_dft_body,
        out_shape=jax.ShapeDtypeStruct((n_layers, n_heads, hd, 2 * hd), BF16),
        grid=(n_layers, n_heads),
        in_specs=[
            pl.BlockSpec((2, hd, hd), lambda l, h: (0, 0, 0)),
            pl.BlockSpec((None, None, hd, hd), lambda l, h: (l, h, 0, 0)),
        ],
        out_specs=pl.BlockSpec((None, None, hd, 2 * hd), lambda l, h: (l, h, 0, 0)),
        compiler_params=pltpu.CompilerParams(dimension_semantics=("arbitrary", "arbitrary")),
        name="head_dft",
    )(chan_cs, fourier_w)


def _mix_in_body(h_ref, win_ref, mcs_ref, pw_ref, pscale_ref, pq_ref, yp_ref, pad_scr, lvl_scr):
    seq, _ = h_ref.shape
    d_f = pq_ref.shape[-1]
    hd = d_f // N_FOURIER_HEADS
    d_p = yp_ref.shape[-1]
    gd = d_p // len(POOL_WINDOWS)
    rc = MIX_ROW_CHUNK
    pad = POOL_PAD_ROWS
    lo, hi = pad // 2, pad + seq + pad // 2

    pad_scr[0:pad, :] = jnp.zeros((pad, d_p), F32)
    pad_scr[pad + seq:pad + seq + pad, :] = jnp.zeros((pad, d_p), F32)
    for buf in range(2):
        lvl_scr[buf, hi:hi + pad // 2, :] = jnp.zeros((pad // 2, d_p), F32)

    half = seq // 2
    for ci in range(half // rc):
        r0 = ci * rc
        pq_halves = []
        for base in (r0, half + r0):
            u = jnp.dot(h_ref[base:base + rc, :], win_ref[...], preferred_element_type=F32)
            pad_scr[pad + base:pad + base + rc, :] = u[:, d_f:]
            pq_halves.append([
                jnp.dot(u[:, hh * hd:(hh + 1) * hd].astype(BF16), mcs_ref[hh],
                        preferred_element_type=F32)
                for hh in range(N_FOURIER_HEADS)])
        for hh in range(N_FOURIER_HEADS):
            cols = slice(hh * hd, (hh + 1) * hd)
            lo_pq, hi_pq = pq_halves[0][hh], pq_halves[1][hh]
            for c, a in enumerate((lo_pq + hi_pq, lo_pq - hi_pq)):
                pq_ref[c, r0:r0 + rc, cols] = a[:, :hd].astype(BF16)
                pq_ref[c, half + r0:half + r0 + rc, cols] = a[:, hd:].astype(BF16)

    for g, win in enumerate(POOL_WINDOWS):
        radius = win // 2
        cols = slice(g * gd, (g + 1) * gd)
        fwd, width, buf = pad_scr, 1, 0
        while width < radius:
            src = fwd if width == 1 else fwd.at[1 - buf]
            lvl_scr[buf, lo:hi, cols] = src[lo:hi, cols] + src[lo + width:hi + width, cols]
            fwd, width, buf = lvl_scr, 2 * width, 1 - buf
        src = fwd if width == 1 else fwd.at[1 - buf]
        for ci in range(seq // rc):
            p0 = pad + ci * rc
            t = ci * rc + lax.broadcasted_iota(jnp.int32, (rc, gd), 0)
            total = (src[p0 - radius:p0 - radius + rc, cols] + src[p0:p0 + rc, cols]
                     + pad_scr[p0 + radius:p0 + radius + rc, cols])
            count = (jnp.minimum(t + radius, seq - 1) - jnp.maximum(t - radius, 0) + 1).astype(F32)
            diff = total / count - pad_scr[p0:p0 + rc, cols]
            y = jnp.dot(diff.astype(BF16), pw_ref[g], preferred_element_type=F32)
            yp_ref[ci * rc:(ci + 1) * rc, cols] = (y * pscale_ref[:, cols]).astype(BF16)


def _mix_in(h3d, win, mcs, pool_w, pool_scale, layer):
    n_b, seq, d = h3d.shape
    d_mix = win.shape[-1]
    n_heads, hd = mcs.shape[1], mcs.shape[2]
    d_f = n_heads * hd
    d_p = d_mix - d_f
    n_groups, gd = pool_w.shape[1], pool_w.shape[2]
    return pl.pallas_call(
        _mix_in_body,
        out_shape=(jax.ShapeDtypeStruct((n_b, 2, seq, d_f), BF16),
                   jax.ShapeDtypeStruct((n_b, seq, d_p), BF16)),
        grid=(n_b,),
        in_specs=[
            pl.BlockSpec((None, seq, d), lambda b: (b, 0, 0)),
            pl.BlockSpec((None, d, d_mix), lambda b: (layer, 0, 0)),
            pl.BlockSpec((None, n_heads, hd, 2 * hd), lambda b: (layer, 0, 0, 0)),
            pl.BlockSpec((None, n_groups, gd, gd), lambda b: (layer, 0, 0, 0)),
            _gain_spec(layer, d_p, 1),
        ],
        out_specs=(pl.BlockSpec((None, 2, seq, d_f), lambda b: (b, 0, 0, 0)),
                   pl.BlockSpec((None, seq, d_p), lambda b: (b, 0, 0))),
        scratch_shapes=[pltpu.VMEM((seq + 2 * POOL_PAD_ROWS, d_p), F32),
                        pltpu.VMEM((2, seq + 2 * POOL_PAD_ROWS, d_p), F32)],
        compiler_params=pltpu.CompilerParams(
            dimension_semantics=("arbitrary",), vmem_limit_bytes=V7X_VMEM_LIMIT_BYTES),
        name="mix_in",
    )(h3d, win, mcs, pool_w, pool_scale)


def _mix_out_body(ab_ref, dft_ref, yp_ref, x_ref, wout_ref, gpost_ref, gnext_ref,
                  xo_ref, ho_ref, il_scr, cat_scr, y_scr, *, n_steps):
    tr, d = x_ref.shape
    d_f = ab_ref.shape[-1]
    n_slabs = il_scr.shape[0]
    slab = il_scr.shape[-1]
    t = pl.program_id(0)

    def finish_previous_step():
        g_post = gpost_ref[...]
        g_next = gnext_ref[...]
        done = []
        for c in range(tr // NORM_ROW_CHUNK):
            rows = slice(c * NORM_ROW_CHUNK, (c + 1) * NORM_ROW_CHUNK)
            x_new, h_new = _residual_and_next_norm(y_scr[rows, :], x_ref[rows, :], g_post, g_next)
            xo_ref[rows, :] = x_new
            ho_ref[rows, :] = h_new
            done.append(x_new[0:BF16_TILE_ROWS, 0:V7X_LANES])
        return done

    def matmuls(done):
        j0 = pl.multiple_of((t % 2) * (tr // 2), tr // 2)
        for c in range(2):
            yf = jnp.dot(dft_ref[c, pl.ds(j0, tr // 2), :], ab_ref[c],
                         preferred_element_type=F32)
            for s in range(n_slabs):
                il_scr[s, pl.ds(c, tr // 2, stride=2), :] = yf[:, s * slab:(s + 1) * slab]
        for s in range(n_slabs):
            cat_scr[:, s * slab:(s + 1) * slab] = il_scr[s].astype(BF16)
        cat_scr[:, d_f:] = yp_ref[...]
        for c, x_done in enumerate(done[:cat_scr.shape[1] // V7X_LANES]):
            corner = (slice(0, BF16_TILE_ROWS), slice(c * V7X_LANES, (c + 1) * V7X_LANES))
            cat_scr[corner] = cat_scr[corner] + _exact_zero_from(x_done).astype(BF16)
        y_scr[...] = jnp.dot(cat_scr[...], wout_ref[...], preferred_element_type=F32)

    @pl.when(t == 0)
    def _():
        y_scr[...] = jnp.zeros(y_scr.shape, F32)

    @pl.when(t < n_steps)
    def _():
        matmuls(finish_previous_step())

    @pl.when(t == n_steps)
    def _():
        finish_previous_step()


def _mix_out(x3d, dft2, ab, yp, wout, gpost, gnext, layer):
    n_b, seq, d = x3d.shape
    d_f = ab.shape[-1]
    d_p = yp.shape[-1]
    tr = MIX_OUT_ROW_TILE
    tiles_per_seq = seq // tr
    assert tiles_per_seq == 2, "the step parity selects the half of the class rows"
    n_steps = n_b * tiles_per_seq

    def cur(t):
        s = jnp.minimum(t, n_steps - 1)
        return s // tiles_per_seq, s % tiles_per_seq

    def prev(t):
        s = jnp.maximum(t - 1, 0)
        return s // tiles_per_seq, s % tiles_per_seq

    def x_map(t):
        b, jh = prev(t)
        return (b, jh, 0)

    return pl.pallas_call(
        functools.partial(_mix_out_body, n_steps=n_steps),
        out_shape=(jax.ShapeDtypeStruct((n_b, seq, d), F32),
                   jax.ShapeDtypeStruct((n_b, seq, d), BF16)),
        grid=(n_steps + 1,),
        in_specs=[
            pl.BlockSpec((None, 2, seq, d_f), lambda t: (cur(t)[0], 0, 0, 0)),
            pl.BlockSpec((2, seq // 2, seq), lambda t: (0, 0, 0)),
            pl.BlockSpec((None, tr, d_p), lambda t: (cur(t)[0], cur(t)[1], 0)),
            pl.BlockSpec((None, tr, d), x_map),
            pl.BlockSpec((None, d_f + d_p, d), lambda t: (layer, 0, 0)),
            _gain_spec(layer, d, 1),
            _gain_spec(layer, d, 1),
        ],
        out_specs=(pl.BlockSpec((None, tr, d), x_map), pl.BlockSpec((None, tr, d), x_map)),
        scratch_shapes=[pltpu.VMEM((d_f // V7X_LANES, tr, V7X_LANES), F32),
                        pltpu.VMEM((tr, d_f + d_p), BF16),
                        pltpu.VMEM((tr, d), F32)],
        compiler_params=pltpu.CompilerParams(
            dimension_semantics=("arbitrary",), vmem_limit_bytes=V7X_VMEM_LIMIT_BYTES),
        name="mix_out",
    )(ab, dft2, yp, x3d, wout, gpost, gnext)


def _dft_cos_sin(n):
    k = np.arange(n, dtype=np.int64)
    ang = 2.0 * np.pi * ((k[:, None] * k[None, :]) % n).astype(np.float64) / n
    return np.cos(ang), np.sin(ang)


@functools.lru_cache(maxsize=None)
def _constants(seq, hd):
    nq = seq // DFT_RADIX
    n2 = np.arange(nq, dtype=np.int64)[None, None, :]
    k = (DFT_RADIX * np.arange(nq, dtype=np.int64)[None, :, None]
         + np.arange(DFT_RADIX, dtype=np.int64)[:, None, None])
    ang = 2.0 * np.pi * ((n2 * k) % seq).astype(np.float64) / seq
    seq_dft = np.concatenate([np.cos(ang), -np.sin(ang)], axis=2).astype(np.float32)
    cc, sc = _dft_cos_sin(hd)
    ortho = 1.0 / np.sqrt(float(seq) * float(hd))
    chan_cs = (np.stack([cc, sc]) * ortho).astype(np.float32)
    return seq_dft, chan_cs


def kernel(x, ffn1_pre_g, ffn1_w_gate, ffn1_w_up, ffn1_w_down, ffn1_post_g, mix_pre_g, w_in, fourier_w, pool_w, pool_scale, w_out, mix_post_g, ffn2_pre_g, ffn2_w_gate, ffn2_w_up, ffn2_w_down, ffn2_post_g):
    n_b, seq, d = x.shape
    n_layers = w_in.shape[0]
    hd = fourier_w.shape[-1]
    seq_dft_np, chan_cs_np = _constants(seq, hd)
    seq_dft = jnp.asarray(seq_dft_np).astype(BF16)
    mcs = _head_dft(jnp.asarray(chan_cs_np), fourier_w)

    def gain(g):
        return g.reshape(n_layers, 1, g.shape[-1])

    f1_pre, f1_post = gain(ffn1_pre_g), gain(ffn1_post_g)
    f2_pre, f2_post = gain(ffn2_pre_g), gain(ffn2_post_g)
    mix_pre, mix_post, pscale = gain(mix_pre_g), gain(mix_post_g), gain(pool_scale)
    w1 = (ffn1_w_gate.astype(BF16), ffn1_w_up.astype(BF16), ffn1_w_down.astype(BF16))
    w2 = (ffn2_w_gate.astype(BF16), ffn2_w_up.astype(BF16), ffn2_w_down.astype(BF16))
    win_b = w_in.astype(BF16)
    wout_b = w_out.astype(BF16)
    pw_b = pool_w.astype(BF16)

    x2 = x.reshape(n_b * seq, d)
    h2 = _prenorm(x2, f1_pre, 0)
    for l in range(n_layers):
        x2, h2 = _ffn(h2, x2, *w1, f1_post, mix_pre, l, l)
        pq, yp = _mix_in(h2.reshape(n_b, seq, d), win_b, mcs, pw_b, pscale, l)
        x3, h3 = _mix_out(x2.reshape(n_b, seq, d), seq_dft, pq, yp, wout_b, mix_post, f2_pre, l)
        x2, h2 = x3.reshape(n_b * seq, d), h3.reshape(n_b * seq, d)
        if l + 1 < n_layers:
            x2, h2 = _ffn(h2, x2, *w2, f2_post, f1_pre, l, l + 1)
        else:
            x2 = _ffn(h2, x2, *w2, f2_post, None, l, None)
    return x2.reshape(n_b, seq, d)
```

```python
import functools

import numpy as np
import jax
import jax.numpy as jnp
from jax import lax
from jax.experimental import pallas as pl
from jax.experimental.pallas import tpu as pltpu

EPS = 1e-6
N_FOURIER_HEADS = 4
POOL_WINDOWS = (2, 4, 8, 16)

V7X_SUBLANES = 8
V7X_LANES = 128
BF16_TILE_ROWS = 2 * V7X_SUBLANES
V7X_VMEM_LIMIT_BYTES = 58 * 1024 * 1024

FFN_TOKEN_TILE = 512
FFN_HIDDEN_CHUNK = 256
FFN_WEIGHT_LOAD_STEPS = 4
NORM_ROW_CHUNK = 128
PRENORM_ROW_TILE = 1024
MIX_ROW_CHUNK = 512
DFT_RADIX = 2
MIX_OUT_ROW_TILE = 1024
POOL_PAD_ROWS = 2 * V7X_SUBLANES

BF16 = jnp.bfloat16
F32 = jnp.float32


def _rms_scale(x):
    return lax.rsqrt(jnp.mean(x * x, axis=-1, keepdims=True) + EPS)


def _residual_and_next_norm(y, x, g_post, g_next):
    x_new = x + y * _rms_scale(y) * g_post
    h_new = None if g_next is None else (x_new * _rms_scale(x_new) * g_next).astype(BF16)
    return x_new, h_new


def _exact_zero_from(v):
    bits = pltpu.bitcast(v, jnp.uint32)
    half_word = jnp.uint32(16)
    zero = lax.shift_right_logical(lax.shift_right_logical(bits, half_word), half_word)
    return pltpu.bitcast(zero, F32)


def _gain_spec(layer, width):
    return pl.BlockSpec((None, 1, width), lambda i: (layer, 0, 0))


def _prenorm_body(x_ref, g_ref, h_ref):
    x = x_ref[...]
    h_ref[...] = (x * _rms_scale(x) * g_ref[...]).astype(BF16)


def _prenorm(x2d, gains, layer):
    n_tok, d = x2d.shape
    tr = PRENORM_ROW_TILE
    return pl.pallas_call(
        _prenorm_body,
        out_shape=jax.ShapeDtypeStruct((n_tok, d), BF16),
        grid=(n_tok // tr,),
        in_specs=[pl.BlockSpec((tr, d), lambda i: (i, 0)), _gain_spec(layer, d)],
        out_specs=pl.BlockSpec((tr, d), lambda i: (i, 0)),
        compiler_params=pltpu.CompilerParams(dimension_semantics=("arbitrary",)),
        name="prenorm",
    )(x2d, gains)


def _ffn_body(*refs, n_tiles, emit_h):
    if emit_h:
        (h_ref, x_ref, wg_ref, wu_ref, wd_ref, gpost_ref, gnext_ref,
         xo_ref, ho_ref, wg_s, wu_s, wd_s, g_scr, y_scr) = refs
    else:
        (h_ref, x_ref, wg_ref, wu_ref, wd_ref, gpost_ref,
         xo_ref, wg_s, wu_s, wd_s, g_scr, y_scr) = refs
        gnext_ref = ho_ref = None
    tm = h_ref.shape[0]
    d_ff = wg_s.shape[1]
    step = pl.program_id(0)
    n_load = 3 * FFN_WEIGHT_LOAD_STEPS
    i = step - n_load

    for m, (w_ref, w_s) in enumerate(((wg_ref, wg_s), (wu_ref, wu_s), (wd_ref, wd_s))):
        rows = w_ref.shape[0]
        first = m * FFN_WEIGHT_LOAD_STEPS

        @pl.when(jnp.logical_and(step >= first, step < first + FFN_WEIGHT_LOAD_STEPS))
        def _(w_ref=w_ref, w_s=w_s, rows=rows, first=first):
            r0 = pl.multiple_of((step - first) * rows, rows)
            w_s[pl.ds(r0, rows), :] = w_ref[...].astype(BF16)

    def finish_previous_tile():
        g_post = 0.5 * gpost_ref[...]
        g_next = gnext_ref[...] if emit_h else None
        done = []
        for c in range(tm // NORM_ROW_CHUNK):
            rows = slice(c * NORM_ROW_CHUNK, (c + 1) * NORM_ROW_CHUNK)
            x_new, h_new = _residual_and_next_norm(y_scr[rows, :], x_ref[rows, :], g_post, g_next)
            xo_ref[rows, :] = x_new
            if emit_h:
                ho_ref[rows, :] = h_new
            done.append(x_new[0:BF16_TILE_ROWS, 0:V7X_LANES])
        return done

    def matmuls(done):
        tie_every = max((d_ff // FFN_HIDDEN_CHUNK) // len(done), 1)
        for c in range(d_ff // FFN_HIDDEN_CHUNK):
            sl = slice(c * FFN_HIDDEN_CHUNK, (c + 1) * FFN_HIDDEN_CHUNK)
            a = jnp.dot(h_ref[...], wg_s[:, sl], preferred_element_type=F32)
            b = jnp.dot(h_ref[...], wu_s[:, sl], preferred_element_type=F32)
            g_scr[:, sl] = ((a / (1.0 + jnp.exp(-a))) * b).astype(BF16)
            if c % tie_every == tie_every - 1 and c // tie_every < len(done):
                corner = (slice(0, BF16_TILE_ROWS),
                          slice(c * FFN_HIDDEN_CHUNK, c * FFN_HIDDEN_CHUNK + V7X_LANES))
                g_scr[corner] = g_scr[corner] + _exact_zero_from(done[c // tie_every]).astype(BF16)
        y_scr[...] = jnp.dot(g_scr[...], wd_s[...], preferred_element_type=F32)

    @pl.when(i == 0)
    def _():
        y_scr[...] = jnp.zeros(y_scr.shape, F32)

    @pl.when(jnp.logical_and(i >= 0, i < n_tiles))
    def _():
        matmuls(finish_previous_tile())

    @pl.when(i == n_tiles)
    def _():
        finish_previous_tile()


def _ffn(h2d, x2d, wg, wu, wd, gpost, gnext, layer, next_layer):
    n_tok, d = x2d.shape
    d_ff = wg.shape[-1]
    tm = FFN_TOKEN_TILE
    n_tiles = n_tok // tm
    emit_h = gnext is not None
    n_ld = FFN_WEIGHT_LOAD_STEPS
    n_load = 3 * n_ld

    def cur(s):
        return (jnp.clip(s - n_load, 0, n_tiles - 1), 0)

    def prev(s):
        return (jnp.clip(s - n_load - 1, 0, n_tiles - 1), 0)

    def weight_chunk(m):
        return lambda s: (layer, jnp.clip(s - m * n_ld, 0, n_ld - 1), 0)

    in_specs = [
        pl.BlockSpec((tm, d), cur),
        pl.BlockSpec((tm, d), prev),
        pl.BlockSpec((None, d // n_ld, d_ff), weight_chunk(0)),
        pl.BlockSpec((None, d // n_ld, d_ff), weight_chunk(1)),
        pl.BlockSpec((None, d_ff // n_ld, d), weight_chunk(2)),
        _gain_spec(layer, d),
    ]
    args = [h2d, x2d, wg, wu, wd, gpost]
    out_shape = [jax.ShapeDtypeStruct((n_tok, d), F32)]
    out_specs = [pl.BlockSpec((tm, d), prev)]
    if emit_h:
        in_specs.append(_gain_spec(next_layer, d))
        args.append(gnext)
        out_shape.append(jax.ShapeDtypeStruct((n_tok, d), BF16))
        out_specs.append(pl.BlockSpec((tm, d), prev))
    out = pl.pallas_call(
        functools.partial(_ffn_body, n_tiles=n_tiles, emit_h=emit_h),
        out_shape=tuple(out_shape),
        grid=(n_load + n_tiles + 1,),
        in_specs=in_specs,
        out_specs=tuple(out_specs),
        scratch_shapes=[pltpu.VMEM((d, d_ff), BF16), pltpu.VMEM((d, d_ff), BF16),
                        pltpu.VMEM((d_ff, d), BF16),
                        pltpu.VMEM((tm, d_ff), BF16), pltpu.VMEM((tm, d), F32)],
        compiler_params=pltpu.CompilerParams(
            dimension_semantics=("arbitrary",), vmem_limit_bytes=V7X_VMEM_LIMIT_BYTES),
        name="ffn",
    )(*args)
    return out if emit_h else out[0]


def _head_dft_body(cs_ref, fw_ref, o_ref):
    hd = fw_ref.shape[0]
    fw = fw_ref[...]
    o_ref[:, :hd] = jnp.dot(cs_ref[0], fw, preferred_element_type=F32,
                            precision=lax.Precision.HIGHEST).astype(BF16)
    o_ref[:, hd:] = jnp.dot(cs_ref[1], fw, preferred_element_type=F32,
                            precision=lax.Precision.HIGHEST).astype(BF16)


def _head_dft(chan_cs, fourier_w):
    n_layers, n_heads, hd, _ = fourier_w.shape
    return pl.pallas_call(
        _head_dft_body,
        out_shape=jax.ShapeDtypeStruct((n_layers, n_heads, hd, 2 * hd), BF16),
        grid=(n_layers, n_heads),
        in_specs=[
            pl.BlockSpec((2, hd, hd), lambda l, h: (0, 0, 0)),
            pl.BlockSpec((None, None, hd, hd), lambda l, h: (l, h, 0, 0)),
        ],
        out_specs=pl.BlockSpec((None, None, hd, 2 * hd), lambda l, h: (l, h, 0, 0)),
        compiler_params=pltpu.CompilerParams(dimension_semantics=("arbitrary", "arbitrary")),
        name="head_dft",
    )(chan_cs, fourier_w)


def _mix_in_body(h_ref, win_ref, mcs_ref, pw_ref, pscale_ref, pq_ref, yp_ref, pad_scr, lvl_scr):
    seq, _ = h_ref.shape
    d_f = pq_ref.shape[-1]
    hd = d_f // N_FOURIER_HEADS
    d_p = yp_ref.shape[-1]
    gd = d_p // len(POOL_WINDOWS)
    rc = MIX_ROW_CHUNK
    pad = POOL_PAD_ROWS
    lo, hi = pad // 2, pad + seq + pad // 2

    pad_scr[0:pad, :] = jnp.zeros((pad, d_p), F32)
    pad_scr[pad + seq:pad + seq + pad, :] = jnp.zeros((pad, d_p), F32)
    for buf in range(2):
        lvl_scr[buf, hi:hi + pad // 2, :] = jnp.zeros((pad // 2, d_p), F32)

    half = seq // 2
    for ci in range(half // rc):
        r0 = ci * rc
        pq_halves = []
        for base in (r0, half + r0):
            u = jnp.dot(h_ref[base:base + rc, :], win_ref[...], preferred_element_type=F32)
            pad_scr[pad + base:pad + base + rc, :] = u[:, d_f:]
            pq_halves.append([
                jnp.dot(u[:, hh * hd:(hh + 1) * hd].astype(BF16), mcs_ref[hh],
                        preferred_element_type=F32)
                for hh in range(N_FOURIER_HEADS)])
        for hh in range(N_FOURIER_HEADS):
            cols = slice(hh * hd, (hh + 1) * hd)
            lo_pq, hi_pq = pq_halves[0][hh], pq_halves[1][hh]
            for c, a in enumerate((lo_pq + hi_pq, lo_pq - hi_pq)):
                pq_ref[c, r0:r0 + rc, cols] = a[:, :hd].astype(BF16)
                pq_ref[c, half + r0:half + r0 + rc, cols] = a[:, hd:].astype(BF16)

    for g, win in enumerate(POOL_WINDOWS):
        radius = win // 2
        cols = slice(g * gd, (g + 1) * gd)
        fwd, width, buf = pad_scr, 1, 0
        while width < radius:
            src = fwd if width == 1 else fwd.at[1 - buf]
            lvl_scr[buf, lo:hi, cols] = src[lo:hi, cols] + src[lo + width:hi + width, cols]
            fwd, width, buf = lvl_scr, 2 * width, 1 - buf
        src = fwd if width == 1 else fwd.at[1 - buf]
        for ci in range(seq // rc):
            p0 = pad + ci * rc
            t = ci * rc + lax.broadcasted_iota(jnp.int32, (rc, gd), 0)
            total = (src[p0 - radius:p0 - radius + rc, cols] + src[p0:p0 + rc, cols]
                     + pad_scr[p0 + radius:p0 + radius + rc, cols])
            count = (jnp.minimum(t + radius, seq - 1) - jnp.maximum(t - radius, 0) + 1).astype(F32)
            diff = total / count - pad_scr[p0:p0 + rc, cols]
            y = jnp.dot(diff.astype(BF16), pw_ref[g], preferred_element_type=F32)
            yp_ref[ci * rc:(ci + 1) * rc, cols] = (y * pscale_ref[:, cols]).astype(BF16)


def _mix_in(h3d, win, mcs, pool_w, pool_scale, layer):
    n_b, seq, d = h3d.shape
    d_mix = win.shape[-1]
    n_heads, hd = mcs.shape[1], mcs.shape[2]
    d_f = n_heads * hd
    d_p = d_mix - d_f
    n_groups, gd = pool_w.shape[1], pool_w.shape[2]
    return pl.pallas_call(
        _mix_in_body,
        out_shape=(jax.ShapeDtypeStruct((n_b, 2, seq, d_f), BF16),
                   jax.ShapeDtypeStruct((n_b, seq, d_p), BF16)),
        grid=(n_b,),
        in_specs=[
            pl.BlockSpec((None, seq, d), lambda b: (b, 0, 0)),
            pl.BlockSpec((None, d, d_mix), lambda b: (layer, 0, 0)),
            pl.BlockSpec((None, n_heads, hd, 2 * hd), lambda b: (layer, 0, 0, 0)),
            pl.BlockSpec((None, n_groups, gd, gd), lambda b: (layer, 0, 0, 0)),
            _gain_spec(layer, d_p),
        ],
        out_specs=(pl.BlockSpec((None, 2, seq, d_f), lambda b: (b, 0, 0, 0)),
                   pl.BlockSpec((None, seq, d_p), lambda b: (b, 0, 0))),
        scratch_shapes=[pltpu.VMEM((seq + 2 * POOL_PAD_ROWS, d_p), F32),
                        pltpu.VMEM((2, seq + 2 * POOL_PAD_ROWS, d_p), F32)],
        compiler_params=pltpu.CompilerParams(
            dimension_semantics=("arbitrary",), vmem_limit_bytes=V7X_VMEM_LIMIT_BYTES),
        name="mix_in",
    )(h3d, win, mcs, pool_w, pool_scale)


def _mix_out_body(ab_ref, dft_ref, yp_ref, x_ref, wout_ref, gpost_ref, gnext_ref,
                  xo_ref, ho_ref, il_scr, cat_scr, y_scr, *, n_steps):
    tr, d = x_ref.shape
    d_f = ab_ref.shape[-1]
    n_slabs = il_scr.shape[0]
    slab = il_scr.shape[-1]
    t = pl.program_id(0)

    def finish_previous_step():
        g_post = gpost_ref[...]
        g_next = gnext_ref[...]
        done = []
        for c in range(tr // NORM_ROW_CHUNK):
            rows = slice(c * NORM_ROW_CHUNK, (c + 1) * NORM_ROW_CHUNK)
            x_new, h_new = _residual_and_next_norm(y_scr[rows, :], x_ref[rows, :], g_post, g_next)
            xo_ref[rows, :] = x_new
            ho_ref[rows, :] = h_new
            done.append(x_new[0:BF16_TILE_ROWS, 0:V7X_LANES])
        return done

    def matmuls(done):
        j0 = pl.multiple_of((t % 2) * (tr // 2), tr // 2)
        for c in range(2):
            yf = jnp.dot(dft_ref[c, pl.ds(j0, tr // 2), :], ab_ref[c],
                         preferred_element_type=F32)
            for s in range(n_slabs):
                il_scr[s, pl.ds(c, tr // 2, stride=2), :] = yf[:, s * slab:(s + 1) * slab]
        for s in range(n_slabs):
            cat_scr[:, s * slab:(s + 1) * slab] = il_scr[s].astype(BF16)
        cat_scr[:, d_f:] = yp_ref[...]
        for c, x_done in enumerate(done[:cat_scr.shape[1] // V7X_LANES]):
            corner = (slice(0, BF16_TILE_ROWS), slice(c * V7X_LANES, (c + 1) * V7X_LANES))
            cat_scr[corner] = cat_scr[corner] + _exact_zero_from(x_done).astype(BF16)
        y_scr[...] = jnp.dot(cat_scr[...], wout_ref[...], preferred_element_type=F32)

    @pl.when(t == 0)
    def _():
        y_scr[...] = jnp.zeros(y_scr.shape, F32)

    @pl.when(t < n_steps)
    def _():
        matmuls(finish_previous_step())

    @pl.when(t == n_steps)
    def _():
        finish_previous_step()


def _mix_out(x3d, dft2, ab, yp, wout, gpost, gnext, layer):
    n_b, seq, d = x3d.shape
    d_f = ab.shape[-1]
    d_p = yp.shape[-1]
    tr = MIX_OUT_ROW_TILE
    tiles_per_seq = seq // tr
    assert tiles_per_seq == 2, "the step parity selects the half of the class rows"
    n_steps = n_b * tiles_per_seq

    def cur(t):
        s = jnp.minimum(t, n_steps - 1)
        return s // tiles_per_seq, s % tiles_per_seq

    def prev(t):
        s = jnp.maximum(t - 1, 0)
        return s // tiles_per_seq, s % tiles_per_seq

    def x_map(t):
        b, jh = prev(t)
        return (b, jh, 0)

    return pl.pallas_call(
        functools.partial(_mix_out_body, n_steps=n_steps),
        out_shape=(jax.ShapeDtypeStruct((n_b, seq, d), F32),
                   jax.ShapeDtypeStruct((n_b, seq, d), BF16)),
        grid=(n_steps + 1,),
        in_specs=[
            pl.BlockSpec((None, 2, seq, d_f), lambda t: (cur(t)[0], 0, 0, 0)),
            pl.BlockSpec((2, seq // 2, seq), lambda t: (0, 0, 0)),
            pl.BlockSpec((None, tr, d_p), lambda t: (cur(t)[0], cur(t)[1], 0)),
            pl.BlockSpec((None, tr, d), x_map),
            pl.BlockSpec((None, d_f + d_p, d), lambda t: (layer, 0, 0)),
            _gain_spec(layer, d),
            _gain_spec(layer, d),
        ],
        out_specs=(pl.BlockSpec((None, tr, d), x_map), pl.BlockSpec((None, tr, d), x_map)),
        scratch_shapes=[pltpu.VMEM((d_f // V7X_LANES, tr, V7X_LANES), F32),
                        pltpu.VMEM((tr, d_f + d_p), BF16),
                        pltpu.VMEM((tr, d), F32)],
        compiler_params=pltpu.CompilerParams(
            dimension_semantics=("arbitrary",), vmem_limit_bytes=V7X_VMEM_LIMIT_BYTES),
        name="mix_out",
    )(ab, dft2, yp, x3d, wout, gpost, gnext)


def _dft_cos_sin(n):
    k = np.arange(n, dtype=np.int64)
    ang = 2.0 * np.pi * ((k[:, None] * k[None, :]) % n).astype(np.float64) / n
    return np.cos(ang), np.sin(ang)


@functools.lru_cache(maxsize=None)
def _constants(seq, hd):
    nq = seq // DFT_RADIX
    n2 = np.arange(nq, dtype=np.int64)[None, None, :]
    k = (DFT_RADIX * np.arange(nq, dtype=np.int64)[None, :, None]
         + np.arange(DFT_RADIX, dtype=np.int64)[:, None, None])
    ang = 2.0 * np.pi * ((n2 * k) % seq).astype(np.float64) / seq
    seq_dft = np.concatenate([np.cos(ang), -np.sin(ang)], axis=2).astype(np.float32)
    cc, sc = _dft_cos_sin(hd)
    ortho = 1.0 / np.sqrt(float(seq) * float(hd))
    chan_cs = (np.stack([cc, sc]) * ortho).astype(np.float32)
    return seq_dft, chan_cs


def kernel(x, ffn1_pre_g, ffn1_w_gate, ffn1_w_up, ffn1_w_down, ffn1_post_g, mix_pre_g, w_in, fourier_w, pool_w, pool_scale, w_out, mix_post_g, ffn2_pre_g, ffn2_w_gate, ffn2_w_up, ffn2_w_down, ffn2_post_g):
    n_b, seq, d = x.shape
    n_layers = w_in.shape[0]
    hd = fourier_w.shape[-1]
    seq_dft_np, chan_cs_np = _constants(seq, hd)
    seq_dft = jnp.asarray(seq_dft_np).astype(BF16)
    mcs = _head_dft(jnp.asarray(chan_cs_np), fourier_w)

    def gain(g):
        return g.reshape(n_layers, 1, g.shape[-1])

    f1_pre, f1_post = gain(ffn1_pre_g), gain(ffn1_post_g)
    f2_pre, f2_post = gain(ffn2_pre_g), gain(ffn2_post_g)
    mix_pre, mix_post, pscale = gain(mix_pre_g), gain(mix_post_g), gain(pool_scale)
    w1 = (ffn1_w_gate, ffn1_w_up, ffn1_w_down)
    w2 = (ffn2_w_gate, ffn2_w_up, ffn2_w_down)
    win_b = w_in.astype(BF16)
    wout_b = w_out.astype(BF16)
    pw_b = pool_w.astype(BF16)

    x2 = x.reshape(n_b * seq, d)
    h2 = _prenorm(x2, f1_pre, 0)
    for l in range(n_layers):
        x2, h2 = _ffn(h2, x2, *w1, f1_post, mix_pre, l, l)
        pq, yp = _mix_in(h2.reshape(n_b, seq, d), win_b, mcs, pw_b, pscale, l)
        x3, h3 = _mix_out(x2.reshape(n_b, seq, d), seq_dft, pq, yp, wout_b, mix_post, f2_pre, l)
        x2, h2 = x3.reshape(n_b * seq, d), h3.reshape(n_b * seq, d)
        if l + 1 < n_layers:
            x2, h2 = _ffn(h2, x2, *w2, f2_post, f1_pre, l, l + 1)
        else:
            x2 = _ffn(h2, x2, *w2, f2_post, None, l, None)
    return x2.reshape(n_b, seq, d)
```

```python
import functools

import numpy as np
import jax
import jax.numpy as jnp
from jax import lax
from jax.experimental import pallas as pl
from jax.experimental.pallas import tpu as pltpu

EPS = 1e-6
N_FOURIER_HEADS = 4
POOL_WINDOWS = (2, 4, 8, 16)

V7X_SUBLANES = 8
V7X_LANES = 128
BF16_TILE_ROWS = 2 * V7X_SUBLANES
V7X_VMEM_LIMIT_BYTES = 58 * 1024 * 1024

FFN_TOKEN_TILE = 512
FFN_HIDDEN_CHUNK = 256
FFN_WEIGHT_LOAD_STEPS = 4
NORM_ROW_CHUNK = 128
MIX_ROW_CHUNK = 512
DFT_RADIX = 2
MIX_OUT_ROW_TILE = 1024
POOL_PAD_ROWS = 2 * V7X_SUBLANES

BF16 = jnp.bfloat16
F32 = jnp.float32


def _rms_scale(x):
    return lax.rsqrt(jnp.mean(x * x, axis=-1, keepdims=True) + EPS)


def _residual_and_next_norm(y, x, g_post, g_next):
    x_new = x + y * _rms_scale(y) * g_post
    h_new = None if g_next is None else (x_new * _rms_scale(x_new) * g_next).astype(BF16)
    return x_new, h_new


def _exact_zero_from(v):
    bits = pltpu.bitcast(v, jnp.uint32)
    half_word = jnp.uint32(16)
    zero = lax.shift_right_logical(lax.shift_right_logical(bits, half_word), half_word)
    return pltpu.bitcast(zero, F32)


def _gain_spec(layer, width):
    return pl.BlockSpec((None, 1, width), lambda i: (layer, 0, 0))


def _ffn_body(*refs, n_tiles, emit_h, pre_norm):
    refs = list(refs)
    hin_ref, x_ref = refs[0], refs[1]
    refs = refs[2:]
    gpre_ref = refs.pop(0) if pre_norm else None
    wg_ref, wu_ref, wd_ref, gpost_ref = refs[:4]
    refs = refs[4:]
    gnext_ref = refs.pop(0) if emit_h else None
    xo_ref = refs.pop(0)
    ho_ref = refs.pop(0) if emit_h else None
    wg_s, wu_s, wd_s, g_scr, y_scr = refs[:5]
    h_ref = refs[5] if pre_norm else hin_ref
    tm = x_ref.shape[0]
    d_ff = wg_s.shape[1]
    step = pl.program_id(0)
    n_load = FFN_WEIGHT_LOAD_STEPS
    i = step - n_load

    @pl.when(step < n_load)
    def _():
        for w_ref, w_s in ((wg_ref, wg_s), (wu_ref, wu_s), (wd_ref, wd_s)):
            rows = w_ref.shape[0]
            r0 = pl.multiple_of(step * rows, rows)
            w_s[pl.ds(r0, rows), :] = w_ref[...].astype(BF16)

    if pre_norm:
        @pl.when(jnp.logical_and(i >= 0, i < n_tiles))
        def _():
            g_pre = gpre_ref[...]
            for c in range(tm // NORM_ROW_CHUNK):
                rows = slice(c * NORM_ROW_CHUNK, (c + 1) * NORM_ROW_CHUNK)
                x = hin_ref[rows, :]
                h_ref[rows, :] = (x * _rms_scale(x) * g_pre).astype(BF16)

    def finish_previous_tile():
        g_post = 0.5 * gpost_ref[...]
        g_next = gnext_ref[...] if emit_h else None
        done = []
        for c in range(tm // NORM_ROW_CHUNK):
            rows = slice(c * NORM_ROW_CHUNK, (c + 1) * NORM_ROW_CHUNK)
            x_new, h_new = _residual_and_next_norm(y_scr[rows, :], x_ref[rows, :], g_post, g_next)
            xo_ref[rows, :] = x_new
            if emit_h:
                ho_ref[rows, :] = h_new
            done.append(x_new[0:BF16_TILE_ROWS, 0:V7X_LANES])
        return done

    def matmuls(done):
        tie_every = max((d_ff // FFN_HIDDEN_CHUNK) // len(done), 1)
        for c in range(d_ff // FFN_HIDDEN_CHUNK):
            sl = slice(c * FFN_HIDDEN_CHUNK, (c + 1) * FFN_HIDDEN_CHUNK)
            a = jnp.dot(h_ref[...], wg_s[:, sl], preferred_element_type=F32)
            b = jnp.dot(h_ref[...], wu_s[:, sl], preferred_element_type=F32)
            g_scr[:, sl] = ((a / (1.0 + jnp.exp(-a))) * b).astype(BF16)
            if c % tie_every == tie_every - 1 and c // tie_every < len(done):
                corner = (slice(0, BF16_TILE_ROWS),
                          slice(c * FFN_HIDDEN_CHUNK, c * FFN_HIDDEN_CHUNK + V7X_LANES))
                g_scr[corner] = g_scr[corner] + _exact_zero_from(done[c // tie_every]).astype(BF16)
        y_scr[...] = jnp.dot(g_scr[...], wd_s[...], preferred_element_type=F32)

    @pl.when(i == 0)
    def _():
        y_scr[...] = jnp.zeros(y_scr.shape, F32)

    @pl.when(jnp.logical_and(i >= 0, i < n_tiles))
    def _():
        matmuls(finish_previous_tile())

    @pl.when(i == n_tiles)
    def _():
        finish_previous_tile()


def _ffn(h2d, x2d, gpre, wg, wu, wd, gpost, gnext, layer, next_layer):
    n_tok, d = x2d.shape
    d_ff = wg.shape[-1]
    tm = FFN_TOKEN_TILE
    n_tiles = n_tok // tm
    emit_h = gnext is not None
    pre_norm = h2d is None
    n_load = FFN_WEIGHT_LOAD_STEPS

    def cur(s):
        return (jnp.clip(s - n_load, 0, n_tiles - 1), 0)

    def prev(s):
        return (jnp.clip(s - n_load - 1, 0, n_tiles - 1), 0)

    def weight_chunk(s):
        return (layer, jnp.minimum(s, n_load - 1), 0)

    in_specs = [pl.BlockSpec((tm, d), cur), pl.BlockSpec((tm, d), prev)]
    args = [x2d if pre_norm else h2d, x2d]
    if pre_norm:
        in_specs.append(_gain_spec(layer, d))
        args.append(gpre)
    in_specs += [
        pl.BlockSpec((None, d // n_load, d_ff), weight_chunk),
        pl.BlockSpec((None, d // n_load, d_ff), weight_chunk),
        pl.BlockSpec((None, d_ff // n_load, d), weight_chunk),
        _gain_spec(layer, d),
    ]
    args += [wg, wu, wd, gpost]
    out_shape = [jax.ShapeDtypeStruct((n_tok, d), F32)]
    out_specs = [pl.BlockSpec((tm, d), prev)]
    if emit_h:
        in_specs.append(_gain_spec(next_layer, d))
        args.append(gnext)
        out_shape.append(jax.ShapeDtypeStruct((n_tok, d), BF16))
        out_specs.append(pl.BlockSpec((tm, d), prev))
    scratch_shapes = [pltpu.VMEM((d, d_ff), BF16), pltpu.VMEM((d, d_ff), BF16),
                      pltpu.VMEM((d_ff, d), BF16),
                      pltpu.VMEM((tm, d_ff), BF16), pltpu.VMEM((tm, d), F32)]
    if pre_norm:
        scratch_shapes.append(pltpu.VMEM((tm, d), BF16))
    out = pl.pallas_call(
        functools.partial(_ffn_body, n_tiles=n_tiles, emit_h=emit_h, pre_norm=pre_norm),
        out_shape=tuple(out_shape),
        grid=(n_load + n_tiles + 1,),
        in_specs=in_specs,
        out_specs=tuple(out_specs),
        scratch_shapes=scratch_shapes,
        compiler_params=pltpu.CompilerParams(
            dimension_semantics=("arbitrary",), vmem_limit_bytes=V7X_VMEM_LIMIT_BYTES),
        name="ffn",
    )(*args)
    return out if emit_h else out[0]


def _head_dft_body(cs_ref, fw_ref, o_ref):
    hd = fw_ref.shape[0]
    fw = fw_ref[...]
    o_ref[:, :hd] = jnp.dot(cs_ref[0], fw, preferred_element_type=F32,
                            precision=lax.Precision.HIGHEST).astype(BF16)
    o_ref[:, hd:] = jnp.dot(cs_ref[1], fw, preferred_element_type=F32,
                            precision=lax.Precision.HIGHEST).astype(BF16)


def _head_dft(chan_cs, fourier_w):
    n_layers, n_heads, hd, _ = fourier_w.shape
    return pl.pallas_call(
        _head_dft_body,
        out_shape=jax.ShapeDtypeStruct((n_layers, n_heads, hd, 2 * hd), BF16),
        grid=(n_layers, n_heads),
        in_specs=[
            pl.BlockSpec((2, hd, hd), lambda l, h: (0, 0, 0)),
            pl.BlockSpec((None, None, hd, hd), lambda l, h: (l, h, 0, 0)),
        ],
        out_specs=pl.BlockSpec((None, None, hd, 2 * hd), lambda l, h: (l, h, 0, 0)),
        compiler_params=pltpu.CompilerParams(dimension_semantics=("arbitrary", "arbitrary")),
        name="head_dft",
    )(chan_cs, fourier_w)


def _mix_in_body(h_ref, win_ref, mcs_ref, pw_ref, pscale_ref, pq_ref, yp_ref, pad_scr, lvl_scr):
    seq, _ = h_ref.shape
    d_f = pq_ref.shape[-1]
    hd = d_f // N_FOURIER_HEADS
    d_p = yp_ref.shape[-1]
    gd = d_p // len(POOL_WINDOWS)
    rc = MIX_ROW_CHUNK
    pad = POOL_PAD_ROWS
    lo, hi = pad // 2, pad + seq + pad // 2

    pad_scr[0:pad, :] = jnp.zeros((pad, d_p), F32)
    pad_scr[pad + seq:pad + seq + pad, :] = jnp.zeros((pad, d_p), F32)
    for buf in range(2):
        lvl_scr[buf, hi:hi + pad // 2, :] = jnp.zeros((pad // 2, d_p), F32)

    half = seq // 2
    for ci in range(half // rc):
        r0 = ci * rc
        pq_halves = []
        for base in (r0, half + r0):
            u = jnp.dot(h_ref[base:base + rc, :], win_ref[...], preferred_element_type=F32)
            pad_scr[pad + base:pad + base + rc, :] = u[:, d_f:]
            pq_halves.append([
                jnp.dot(u[:, hh * hd:(hh + 1) * hd].astype(BF16), mcs_ref[hh],
                        preferred_element_type=F32)
                for hh in range(N_FOURIER_HEADS)])
        for hh in range(N_FOURIER_HEADS):
            cols = slice(hh * hd, (hh + 1) * hd)
            lo_pq, hi_pq = pq_halves[0][hh], pq_halves[1][hh]
            for c, a in enumerate((lo_pq + hi_pq, lo_pq - hi_pq)):
                pq_ref[c, r0:r0 + rc, cols] = a[:, :hd].astype(BF16)
                pq_ref[c, half + r0:half + r0 + rc, cols] = a[:, hd:].astype(BF16)

    for g, win in enumerate(POOL_WINDOWS):
        radius = win // 2
        cols = slice(g * gd, (g + 1) * gd)
        fwd, width, buf = pad_scr, 1, 0
        while width < radius:
            src = fwd if width == 1 else fwd.at[1 - buf]
            lvl_scr[buf, lo:hi, cols] = src[lo:hi, cols] + src[lo + width:hi + width, cols]
            fwd, width, buf = lvl_scr, 2 * width, 1 - buf
        src = fwd if width == 1 else fwd.at[1 - buf]
        for ci in range(seq // rc):
            p0 = pad + ci * rc
            t = ci * rc + lax.broadcasted_iota(jnp.int32, (rc, gd), 0)
            total = (src[p0 - radius:p0 - radius + rc, cols] + src[p0:p0 + rc, cols]
                     + pad_scr[p0 + radius:p0 + radius + rc, cols])
            count = (jnp.minimum(t + radius, seq - 1) - jnp.maximum(t - radius, 0) + 1).astype(F32)
            diff = total / count - pad_scr[p0:p0 + rc, cols]
            y = jnp.dot(diff.astype(BF16), pw_ref[g], preferred_element_type=F32)
            yp_ref[ci * rc:(ci + 1) * rc, cols] = (y * pscale_ref[:, cols]).astype(BF16)


def _mix_in(h3d, win, mcs, pool_w, pool_scale, layer):
    n_b, seq, d = h3d.shape
    d_mix = win.shape[-1]
    n_heads, hd = mcs.shape[1], mcs.shape[2]
    d_f = n_heads * hd
    d_p = d_mix - d_f
    n_groups, gd = pool_w.shape[1], pool_w.shape[2]
    return pl.pallas_call(
        _mix_in_body,
        out_shape=(jax.ShapeDtypeStruct((n_b, 2, seq, d_f), BF16),
                   jax.ShapeDtypeStruct((n_b, seq, d_p), BF16)),
        grid=(n_b,),
        in_specs=[
            pl.BlockSpec((None, seq, d), lambda b: (b, 0, 0)),
            pl.BlockSpec((None, d, d_mix), lambda b: (layer, 0, 0)),
            pl.BlockSpec((None, n_heads, hd, 2 * hd), lambda b: (layer, 0, 0, 0)),
            pl.BlockSpec((None, n_groups, gd, gd), lambda b: (layer, 0, 0, 0)),
            _gain_spec(layer, d_p),
        ],
        out_specs=(pl.BlockSpec((None, 2, seq, d_f), lambda b: (b, 0, 0, 0)),
                   pl.BlockSpec((None, seq, d_p), lambda b: (b, 0, 0))),
        scratch_shapes=[pltpu.VMEM((seq + 2 * POOL_PAD_ROWS, d_p), F32),
                        pltpu.VMEM((2, seq + 2 * POOL_PAD_ROWS, d_p), F32)],
        compiler_params=pltpu.CompilerParams(
            dimension_semantics=("arbitrary",), vmem_limit_bytes=V7X_VMEM_LIMIT_BYTES),
        name="mix_in",
    )(h3d, win, mcs, pool_w, pool_scale)


def _mix_out_body(ab_ref, dft_ref, yp_ref, x_ref, wout_ref, gpost_ref, gnext_ref,
                  xo_ref, ho_ref, il_scr, cat_scr, y_scr, *, n_steps):
    tr, d = x_ref.shape
    d_f = ab_ref.shape[-1]
    n_slabs = il_scr.shape[0]
    slab = il_scr.shape[-1]
    t = pl.program_id(0)

    def finish_previous_step():
        g_post = gpost_ref[...]
        g_next = gnext_ref[...]
        done = []
        for c in range(tr // NORM_ROW_CHUNK):
            rows = slice(c * NORM_ROW_CHUNK, (c + 1) * NORM_ROW_CHUNK)
            x_new, h_new = _residual_and_next_norm(y_scr[rows, :], x_ref[rows, :], g_post, g_next)
            xo_ref[rows, :] = x_new
            ho_ref[rows, :] = h_new
            done.append(x_new[0:BF16_TILE_ROWS, 0:V7X_LANES])
        return done

    def matmuls(done):
        j0 = pl.multiple_of((t % 2) * (tr // 2), tr // 2)
        for c in range(2):
            yf = jnp.dot(dft_ref[c, pl.ds(j0, tr // 2), :], ab_ref[c],
                         preferred_element_type=F32)
            for s in range(n_slabs):
                il_scr[s, pl.ds(c, tr // 2, stride=2), :] = yf[:, s * slab:(s + 1) * slab]
        for s in range(n_slabs):
            cat_scr[:, s * slab:(s + 1) * slab] = il_scr[s].astype(BF16)
        cat_scr[:, d_f:] = yp_ref[...]
        for c, x_done in enumerate(done[:cat_scr.shape[1] // V7X_LANES]):
            corner = (slice(0, BF16_TILE_ROWS), slice(c * V7X_LANES, (c + 1) * V7X_LANES))
            cat_scr[corner] = cat_scr[corner] + _exact_zero_from(x_done).astype(BF16)
        y_scr[...] = jnp.dot(cat_scr[...], wout_ref[...], preferred_element_type=F32)

    @pl.when(t == 0)
    def _():
        y_scr[...] = jnp.zeros(y_scr.shape, F32)

    @pl.when(t < n_steps)
    def _():
        matmuls(finish_previous_step())

    @pl.when(t == n_steps)
    def _():
        finish_previous_step()


def _mix_out(x3d, dft2, ab, yp, wout, gpost, gnext, layer):
    n_b, seq, d = x3d.shape
    d_f = ab.shape[-1]
    d_p = yp.shape[-1]
    tr = MIX_OUT_ROW_TILE
    tiles_per_seq = seq // tr
    assert tiles_per_seq == 2, "the step parity selects the half of the class rows"
    n_steps = n_b * tiles_per_seq

    def cur(t):
        s = jnp.minimum(t, n_steps - 1)
        return s // tiles_per_seq, s % tiles_per_seq

    def prev(t):
        s = jnp.maximum(t - 1, 0)
        return s // tiles_per_seq, s % tiles_per_seq

    def x_map(t):
        b, jh = prev(t)
        return (b, jh, 0)

    return pl.pallas_call(
        functools.partial(_mix_out_body, n_steps=n_steps),
        out_shape=(jax.ShapeDtypeStruct((n_b, seq, d), F32),
                   jax.ShapeDtypeStruct((n_b, seq, d), BF16)),
        grid=(n_steps + 1,),
        in_specs=[
            pl.BlockSpec((None, 2, seq, d_f), lambda t: (cur(t)[0], 0, 0, 0)),
            pl.BlockSpec((2, seq // 2, seq), lambda t: (0, 0, 0)),
            pl.BlockSpec((None, tr, d_p), lambda t: (cur(t)[0], cur(t)[1], 0)),
            pl.BlockSpec((None, tr, d), x_map),
            pl.BlockSpec((None, d_f + d_p, d), lambda t: (layer, 0, 0)),
            _gain_spec(layer, d),
            _gain_spec(layer, d),
        ],
        out_specs=(pl.BlockSpec((None, tr, d), x_map), pl.BlockSpec((None, tr, d), x_map)),
        scratch_shapes=[pltpu.VMEM((d_f // V7X_LANES, tr, V7X_LANES), F32),
                        pltpu.VMEM((tr, d_f + d_p), BF16),
                        pltpu.VMEM((tr, d), F32)],
        compiler_params=pltpu.CompilerParams(
            dimension_semantics=("arbitrary",), vmem_limit_bytes=V7X_VMEM_LIMIT_BYTES),
        name="mix_out",
    )(ab, dft2, yp, x3d, wout, gpost, gnext)


def _dft_cos_sin(n):
    k = np.arange(n, dtype=np.int64)
    ang = 2.0 * np.pi * ((k[:, None] * k[None, :]) % n).astype(np.float64) / n
    return np.cos(ang), np.sin(ang)


@functools.lru_cache(maxsize=None)
def _constants(seq, hd):
    nq = seq // DFT_RADIX
    n2 = np.arange(nq, dtype=np.int64)[None, None, :]
    k = (DFT_RADIX * np.arange(nq, dtype=np.int64)[None, :, None]
         + np.arange(DFT_RADIX, dtype=np.int64)[:, None, None])
    ang = 2.0 * np.pi * ((n2 * k) % seq).astype(np.float64) / seq
    seq_dft = np.concatenate([np.cos(ang), -np.sin(ang)], axis=2).astype(np.float32)
    cc, sc = _dft_cos_sin(hd)
    ortho = 1.0 / np.sqrt(float(seq) * float(hd))
    chan_cs = (np.stack([cc, sc]) * ortho).astype(np.float32)
    return seq_dft, chan_cs


def kernel(x, ffn1_pre_g, ffn1_w_gate, ffn1_w_up, ffn1_w_down, ffn1_post_g, mix_pre_g, w_in, fourier_w, pool_w, pool_scale, w_out, mix_post_g, ffn2_pre_g, ffn2_w_gate, ffn2_w_up, ffn2_w_down, ffn2_post_g):
    n_b, seq, d = x.shape
    n_layers = w_in.shape[0]
    hd = fourier_w.shape[-1]
    seq_dft_np, chan_cs_np = _constants(seq, hd)
    seq_dft = jnp.asarray(seq_dft_np).astype(BF16)
    mcs = _head_dft(jnp.asarray(chan_cs_np), fourier_w)

    def gain(g):
        return g.reshape(n_layers, 1, g.shape[-1])

    f1_pre, f1_post = gain(ffn1_pre_g), gain(ffn1_post_g)
    f2_pre, f2_post = gain(ffn2_pre_g), gain(ffn2_post_g)
    mix_pre, mix_post, pscale = gain(mix_pre_g), gain(mix_post_g), gain(pool_scale)
    w1 = (ffn1_w_gate, ffn1_w_up, ffn1_w_down)
    w2 = (ffn2_w_gate, ffn2_w_up, ffn2_w_down)
    win_b = w_in.astype(BF16)
    wout_b = w_out.astype(BF16)
    pw_b = pool_w.astype(BF16)

    x2 = x.reshape(n_b * seq, d)
    h2 = None
    for l in range(n_layers):
        x2, h2 = _ffn(h2, x2, f1_pre, *w1, f1_post, mix_pre, l, l)
        pq, yp = _mix_in(h2.reshape(n_b, seq, d), win_b, mcs, pw_b, pscale, l)
        x3, h3 = _mix_out(x2.reshape(n_b, seq, d), seq_dft, pq, yp, wout_b, mix_post, f2_pre, l)
        x2, h2 = x3.reshape(n_b * seq, d), h3.reshape(n_b * seq, d)
        if l + 1 < n_layers:
            x2, h2 = _ffn(h2, x2, None, *w2, f2_post, f1_pre, l, l + 1)
        else:
            x2 = _ffn(h2, x2, None, *w2, f2_post, None, l, None)
    return x2.reshape(n_b, seq, d)
```

```python
import functools

import numpy as np
import jax
import jax.numpy as jnp
from jax import lax
from jax.experimental import pallas as pl
from jax.experimental.pallas import tpu as pltpu

EPS = 1e-6
N_FOURIER_HEADS = 4
POOL_WINDOWS = (2, 4, 8, 16)

V7X_SUBLANES = 8
V7X_LANES = 128
BF16_TILE_ROWS = 2 * V7X_SUBLANES
V7X_VMEM_LIMIT_BYTES = 58 * 1024 * 1024

FFN_TOKEN_TILE = 512
FFN_HIDDEN_CHUNK = 256
FFN_WEIGHT_LOAD_STEPS = 4
NORM_ROW_CHUNK = 128
MIX_ROW_CHUNK = 512
DFT_RADIX = 2
MIX_OUT_ROW_TILE = 1024
POOL_PAD_ROWS = 2 * V7X_SUBLANES

BF16 = jnp.bfloat16
F32 = jnp.float32


def _rms_scale(x):
    return lax.rsqrt(jnp.mean(x * x, axis=-1, keepdims=True) + EPS)


def _residual_and_next_norm(y, x, g_post, g_next):
    x_new = x + y * _rms_scale(y) * g_post
    h_new = None if g_next is None else (x_new * _rms_scale(x_new) * g_next).astype(BF16)
    return x_new, h_new


def _exact_zero_from(v):
    bits = pltpu.bitcast(v, jnp.uint32)
    half_word = jnp.uint32(16)
    zero = lax.shift_right_logical(lax.shift_right_logical(bits, half_word), half_word)
    return pltpu.bitcast(zero, F32)


def _gain_spec(layer, width):
    return pl.BlockSpec((None, 1, width), lambda i: (layer, 0, 0))


def _ffn_body(*refs, n_tiles, emit_h, pre_norm):
    refs = list(refs)
    hin_ref, x_ref = refs[0], refs[1]
    refs = refs[2:]
    gpre_ref = refs.pop(0) if pre_norm else None
    wg_ref, wu_ref, wd_ref, gpost_ref = refs[:4]
    refs = refs[4:]
    gnext_ref = refs.pop(0) if emit_h else None
    xo_ref = refs.pop(0)
    ho_ref = refs.pop(0) if emit_h else None
    wg_s, wu_s, wd_s, g_scr, y_scr = refs[:5]
    h_bufs = tuple(refs[5:7]) if pre_norm else None
    tm = x_ref.shape[0]
    d_ff = wg_s.shape[1]
    step = pl.program_id(0)
    n_load = FFN_WEIGHT_LOAD_STEPS
    i = step - n_load

    @pl.when(step < n_load)
    def _():
        for w_ref, w_s in ((wg_ref, wg_s), (wu_ref, wu_s), (wd_ref, wd_s)):
            rows = w_ref.shape[0]
            r0 = pl.multiple_of(step * rows, rows)
            w_s[pl.ds(r0, rows), :] = w_ref[...].astype(BF16)

    def pre_norm_into(h_dst, src_ref):
        g_pre = gpre_ref[...]
        done = []
        for c in range(tm // NORM_ROW_CHUNK):
            rows = slice(c * NORM_ROW_CHUNK, (c + 1) * NORM_ROW_CHUNK)
            x = src_ref[rows, :]
            h = x * _rms_scale(x) * g_pre
            h_dst[rows, :] = h.astype(BF16)
            done.append(h[0:BF16_TILE_ROWS, 0:V7X_LANES])
        return done

    def finish_previous_tile():
        g_post = 0.5 * gpost_ref[...]
        g_next = gnext_ref[...] if emit_h else None
        done = []
        for c in range(tm // NORM_ROW_CHUNK):
            rows = slice(c * NORM_ROW_CHUNK, (c + 1) * NORM_ROW_CHUNK)
            x_new, h_new = _residual_and_next_norm(y_scr[rows, :], x_ref[rows, :], g_post, g_next)
            xo_ref[rows, :] = x_new
            if emit_h:
                ho_ref[rows, :] = h_new
            done.append(x_new[0:BF16_TILE_ROWS, 0:V7X_LANES])
        return done

    def matmuls(h_ref, done):
        tie_every = max((d_ff // FFN_HIDDEN_CHUNK) // len(done), 1)
        for c in range(d_ff // FFN_HIDDEN_CHUNK):
            sl = slice(c * FFN_HIDDEN_CHUNK, (c + 1) * FFN_HIDDEN_CHUNK)
            a = jnp.dot(h_ref[...], wg_s[:, sl], preferred_element_type=F32)
            b = jnp.dot(h_ref[...], wu_s[:, sl], preferred_element_type=F32)
            g_scr[:, sl] = ((a / (1.0 + jnp.exp(-a))) * b).astype(BF16)
            if c % tie_every == tie_every - 1 and c // tie_every < len(done):
                corner = (slice(0, BF16_TILE_ROWS),
                          slice(c * FFN_HIDDEN_CHUNK, c * FFN_HIDDEN_CHUNK + V7X_LANES))
                g_scr[corner] = g_scr[corner] + _exact_zero_from(done[c // tie_every]).astype(BF16)
        y_scr[...] = jnp.dot(g_scr[...], wd_s[...], preferred_element_type=F32)

    @pl.when(i == 0)
    def _():
        y_scr[...] = jnp.zeros(y_scr.shape, F32)
        if pre_norm:
            pre_norm_into(h_bufs[0], x_ref)

    if pre_norm:
        for parity in range(2):
            @pl.when(jnp.logical_and(jnp.logical_and(i >= 0, i < n_tiles), i % 2 == parity))
            def _(parity=parity):
                done = finish_previous_tile() + pre_norm_into(h_bufs[1 - parity], hin_ref)
                matmuls(h_bufs[parity], done)
    else:
        @pl.when(jnp.logical_and(i >= 0, i < n_tiles))
        def _():
            matmuls(hin_ref, finish_previous_tile())

    @pl.when(i == n_tiles)
    def _():
        finish_previous_tile()


def _ffn(h2d, x2d, gpre, wg, wu, wd, gpost, gnext, layer, next_layer):
    n_tok, d = x2d.shape
    d_ff = wg.shape[-1]
    tm = FFN_TOKEN_TILE
    n_tiles = n_tok // tm
    emit_h = gnext is not None
    pre_norm = h2d is None
    n_load = FFN_WEIGHT_LOAD_STEPS

    def tile(offset):
        return lambda s: (jnp.clip(s - n_load + offset, 0, n_tiles - 1), 0)

    prev = tile(-1)

    def weight_chunk(s):
        return (layer, jnp.minimum(s, n_load - 1), 0)

    in_specs = [pl.BlockSpec((tm, d), tile(1 if pre_norm else 0)), pl.BlockSpec((tm, d), prev)]
    args = [x2d if pre_norm else h2d, x2d]
    if pre_norm:
        in_specs.append(_gain_spec(layer, d))
        args.append(gpre)
    in_specs += [
        pl.BlockSpec((None, d // n_load, d_ff), weight_chunk),
        pl.BlockSpec((None, d // n_load, d_ff), weight_chunk),
        pl.BlockSpec((None, d_ff // n_load, d), weight_chunk),
        _gain_spec(layer, d),
    ]
    args += [wg, wu, wd, gpost]
    out_shape = [jax.ShapeDtypeStruct((n_tok, d), F32)]
    out_specs = [pl.BlockSpec((tm, d), prev)]
    if emit_h:
        in_specs.append(_gain_spec(next_layer, d))
        args.append(gnext)
        out_shape.append(jax.ShapeDtypeStruct((n_tok, d), BF16))
        out_specs.append(pl.BlockSpec((tm, d), prev))
    scratch_shapes = [pltpu.VMEM((d, d_ff), BF16), pltpu.VMEM((d, d_ff), BF16),
                      pltpu.VMEM((d_ff, d), BF16),
                      pltpu.VMEM((tm, d_ff), BF16), pltpu.VMEM((tm, d), F32)]
    if pre_norm:
        scratch_shapes += [pltpu.VMEM((tm, d), BF16), pltpu.VMEM((tm, d), BF16)]
    out = pl.pallas_call(
        functools.partial(_ffn_body, n_tiles=n_tiles, emit_h=emit_h, pre_norm=pre_norm),
        out_shape=tuple(out_shape),
        grid=(n_load + n_tiles + 1,),
        in_specs=in_specs,
        out_specs=tuple(out_specs),
        scratch_shapes=scratch_shapes,
        compiler_params=pltpu.CompilerParams(
            dimension_semantics=("arbitrary",), vmem_limit_bytes=V7X_VMEM_LIMIT_BYTES),
        name="ffn",
    )(*args)
    return out if emit_h else out[0]


def _head_dft_body(cs_ref, fw_ref, o_ref):
    hd = fw_ref.shape[0]
    fw = fw_ref[...]
    o_ref[:, :hd] = jnp.dot(cs_ref[0], fw, preferred_element_type=F32,
                            precision=lax.Precision.HIGHEST).astype(BF16)
    o_ref[:, hd:] = jnp.dot(cs_ref[1], fw, preferred_element_type=F32,
                            precision=lax.Precision.HIGHEST).astype(BF16)


def _head_dft(chan_cs, fourier_w):
    n_layers, n_heads, hd, _ = fourier_w.shape
    return pl.pallas_call(
        _head_dft_body,
        out_shape=jax.ShapeDtypeStruct((n_layers, n_heads, hd, 2 * hd), BF16),
        grid=(n_layers, n_heads),
        in_specs=[
            pl.BlockSpec((2, hd, hd), lambda l, h: (0, 0, 0)),
            pl.BlockSpec((None, None, hd, hd), lambda l, h: (l, h, 0, 0)),
        ],
        out_specs=pl.BlockSpec((None, None, hd, 2 * hd), lambda l, h: (l, h, 0, 0)),
        compiler_params=pltpu.CompilerParams(dimension_semantics=("arbitrary", "arbitrary")),
        name="head_dft",
    )(chan_cs, fourier_w)


def _mix_in_body(h_ref, win_ref, mcs_ref, pw_ref, pscale_ref, pq_ref, yp_ref, pad_scr, lvl_scr):
    seq, _ = h_ref.shape
    d_f = pq_ref.shape[-1]
    hd = d_f // N_FOURIER_HEADS
    d_p = yp_ref.shape[-1]
    gd = d_p // len(POOL_WINDOWS)
    rc = MIX_ROW_CHUNK
    pad = POOL_PAD_ROWS
    lo, hi = pad // 2, pad + seq + pad // 2

    pad_scr[0:pad, :] = jnp.zeros((pad, d_p), F32)
    pad_scr[pad + seq:pad + seq + pad, :] = jnp.zeros((pad, d_p), F32)
    for buf in range(2):
        lvl_scr[buf, hi:hi + pad // 2, :] = jnp.zeros((pad // 2, d_p), F32)

    half = seq // 2
    for ci in range(half // rc):
        r0 = ci * rc
        pq_halves = []
        for base in (r0, half + r0):
            u = jnp.dot(h_ref[base:base + rc, :], win_ref[...], preferred_element_type=F32)
            pad_scr[pad + base:pad + base + rc, :] = u[:, d_f:]
            pq_halves.append([
                jnp.dot(u[:, hh * hd:(hh + 1) * hd].astype(BF16), mcs_ref[hh],
                        preferred_element_type=F32)
                for hh in range(N_FOURIER_HEADS)])
        for hh in range(N_FOURIER_HEADS):
            cols = slice(hh * hd, (hh + 1) * hd)
            lo_pq, hi_pq = pq_halves[0][hh], pq_halves[1][hh]
            for c, a in enumerate((lo_pq + hi_pq, lo_pq - hi_pq)):
                pq_ref[c, r0:r0 + rc, cols] = a[:, :hd].astype(BF16)
                pq_ref[c, half + r0:half + r0 + rc, cols] = a[:, hd:].astype(BF16)

    for g, win in enumerate(POOL_WINDOWS):
        radius = win // 2
        cols = slice(g * gd, (g + 1) * gd)
        fwd, width, buf = pad_scr, 1, 0
        while width < radius:
            src = fwd if width == 1 else fwd.at[1 - buf]
            lvl_scr[buf, lo:hi, cols] = src[lo:hi, cols] + src[lo + width:hi + width, cols]
            fwd, width, buf = lvl_scr, 2 * width, 1 - buf
        src = fwd if width == 1 else fwd.at[1 - buf]
        for ci in range(seq // rc):
            p0 = pad + ci * rc
            t = ci * rc + lax.broadcasted_iota(jnp.int32, (rc, gd), 0)
            total = (src[p0 - radius:p0 - radius + rc, cols] + src[p0:p0 + rc, cols]
                     + pad_scr[p0 + radius:p0 + radius + rc, cols])
            count = (jnp.minimum(t + radius, seq - 1) - jnp.maximum(t - radius, 0) + 1).astype(F32)
            diff = total / count - pad_scr[p0:p0 + rc, cols]
            y = jnp.dot(diff.astype(BF16), pw_ref[g], preferred_element_type=F32)
            yp_ref[ci * rc:(ci + 1) * rc, cols] = (y * pscale_ref[:, cols]).astype(BF16)


def _mix_in(h3d, win, mcs, pool_w, pool_scale, layer):
    n_b, seq, d = h3d.shape
    d_mix = win.shape[-1]
    n_heads, hd = mcs.shape[1], mcs.shape[2]
    d_f = n_heads * hd
    d_p = d_mix - d_f
    n_groups, gd = pool_w.shape[1], pool_w.shape[2]
    return pl.pallas_call(
        _mix_in_body,
        out_shape=(jax.ShapeDtypeStruct((n_b, 2, seq, d_f), BF16),
                   jax.ShapeDtypeStruct((n_b, seq, d_p), BF16)),
        grid=(n_b,),
        in_specs=[
            pl.BlockSpec((None, seq, d), lambda b: (b, 0, 0)),
            pl.BlockSpec((None, d, d_mix), lambda b: (layer, 0, 0)),
            pl.BlockSpec((None, n_heads, hd, 2 * hd), lambda b: (layer, 0, 0, 0)),
            pl.BlockSpec((None, n_groups, gd, gd), lambda b: (layer, 0, 0, 0)),
            _gain_spec(layer, d_p),
        ],
        out_specs=(pl.BlockSpec((None, 2, seq, d_f), lambda b: (b, 0, 0, 0)),
                   pl.BlockSpec((None, seq, d_p), lambda b: (b, 0, 0))),
        scratch_shapes=[pltpu.VMEM((seq + 2 * POOL_PAD_ROWS, d_p), F32),
                        pltpu.VMEM((2, seq + 2 * POOL_PAD_ROWS, d_p), F32)],
        compiler_params=pltpu.CompilerParams(
            dimension_semantics=("arbitrary",), vmem_limit_bytes=V7X_VMEM_LIMIT_BYTES),
        name="mix_in",
    )(h3d, win, mcs, pool_w, pool_scale)


def _mix_out_body(ab_ref, dft_ref, yp_ref, x_ref, wout_ref, gpost_ref,
                  xo_ref, il_scr, cat_scr, y_scr, *, n_steps):
    tr, d = x_ref.shape
    d_f = ab_ref.shape[-1]
    n_slabs = il_scr.shape[0]
    slab = il_scr.shape[-1]
    t = pl.program_id(0)

    def finish_previous_step():
        g_post = gpost_ref[...]
        done = []
        for c in range(tr // NORM_ROW_CHUNK):
            rows = slice(c * NORM_ROW_CHUNK, (c + 1) * NORM_ROW_CHUNK)
            x_new, _ = _residual_and_next_norm(y_scr[rows, :], x_ref[rows, :], g_post, None)
            xo_ref[rows, :] = x_new
            done.append(x_new[0:BF16_TILE_ROWS, 0:V7X_LANES])
        return done

    def matmuls(done):
        j0 = pl.multiple_of((t % 2) * (tr // 2), tr // 2)
        for c in range(2):
            yf = jnp.dot(dft_ref[c, pl.ds(j0, tr // 2), :], ab_ref[c],
                         preferred_element_type=F32)
            for s in range(n_slabs):
                il_scr[s, pl.ds(c, tr // 2, stride=2), :] = yf[:, s * slab:(s + 1) * slab]
        for s in range(n_slabs):
            cat_scr[:, s * slab:(s + 1) * slab] = il_scr[s].astype(BF16)
        cat_scr[:, d_f:] = yp_ref[...]
        for c, x_done in enumerate(done[:cat_scr.shape[1] // V7X_LANES]):
            corner = (slice(0, BF16_TILE_ROWS), slice(c * V7X_LANES, (c + 1) * V7X_LANES))
            cat_scr[corner] = cat_scr[corner] + _exact_zero_from(x_done).astype(BF16)
        y_scr[...] = jnp.dot(cat_scr[...], wout_ref[...], preferred_element_type=F32)

    @pl.when(t == 0)
    def _():
        y_scr[...] = jnp.zeros(y_scr.shape, F32)

    @pl.when(t < n_steps)
    def _():
        matmuls(finish_previous_step())

    @pl.when(t == n_steps)
    def _():
        finish_previous_step()


def _mix_out(x3d, dft2, ab, yp, wout, gpost, layer):
    n_b, seq, d = x3d.shape
    d_f = ab.shape[-1]
    d_p = yp.shape[-1]
    tr = MIX_OUT_ROW_TILE
    tiles_per_seq = seq // tr
    assert tiles_per_seq == 2, "the step parity selects the half of the class rows"
    n_steps = n_b * tiles_per_seq

    def cur(t):
        s = jnp.minimum(t, n_steps - 1)
        return s // tiles_per_seq, s % tiles_per_seq

    def prev(t):
        s = jnp.maximum(t - 1, 0)
        return s // tiles_per_seq, s % tiles_per_seq

    def x_map(t):
        b, jh = prev(t)
        return (b, jh, 0)

    return pl.pallas_call(
        functools.partial(_mix_out_body, n_steps=n_steps),
        out_shape=jax.ShapeDtypeStruct((n_b, seq, d), F32),
        grid=(n_steps + 1,),
        in_specs=[
            pl.BlockSpec((None, 2, seq, d_f), lambda t: (cur(t)[0], 0, 0, 0)),
            pl.BlockSpec((2, seq // 2, seq), lambda t: (0, 0, 0)),
            pl.BlockSpec((None, tr, d_p), lambda t: (cur(t)[0], cur(t)[1], 0)),
            pl.BlockSpec((None, tr, d), x_map),
            pl.BlockSpec((None, d_f + d_p, d), lambda t: (layer, 0, 0)),
            _gain_spec(layer, d),
        ],
        out_specs=pl.BlockSpec((None, tr, d), x_map),
        scratch_shapes=[pltpu.VMEM((d_f // V7X_LANES, tr, V7X_LANES), F32),
                        pltpu.VMEM((tr, d_f + d_p), BF16),
                        pltpu.VMEM((tr, d), F32)],
        compiler_params=pltpu.CompilerParams(
            dimension_semantics=("arbitrary",), vmem_limit_bytes=V7X_VMEM_LIMIT_BYTES),
        name="mix_out",
    )(ab, dft2, yp, x3d, wout, gpost)


def _dft_cos_sin(n):
    k = np.arange(n, dtype=np.int64)
    ang = 2.0 * np.pi * ((k[:, None] * k[None, :]) % n).astype(np.float64) / n
    return np.cos(ang), np.sin(ang)


@functools.lru_cache(maxsize=None)
def _constants(seq, hd):
    nq = seq // DFT_RADIX
    n2 = np.arange(nq, dtype=np.int64)[None, None, :]
    k = (DFT_RADIX * np.arange(nq, dtype=np.int64)[None, :, None]
         + np.arange(DFT_RADIX, dtype=np.int64)[:, None, None])
    ang = 2.0 * np.pi * ((n2 * k) % seq).astype(np.float64) / seq
    seq_dft = np.concatenate([np.cos(ang), -np.sin(ang)], axis=2).astype(np.float32)
    cc, sc = _dft_cos_sin(hd)
    ortho = 1.0 / np.sqrt(float(seq) * float(hd))
    chan_cs = (np.stack([cc, sc]) * ortho).astype(np.float32)
    return seq_dft, chan_cs


def kernel(x, ffn1_pre_g, ffn1_w_gate, ffn1_w_up, ffn1_w_down, ffn1_post_g, mix_pre_g, w_in, fourier_w, pool_w, pool_scale, w_out, mix_post_g, ffn2_pre_g, ffn2_w_gate, ffn2_w_up, ffn2_w_down, ffn2_post_g):
    n_b, seq, d = x.shape
    n_layers = w_in.shape[0]
    hd = fourier_w.shape[-1]
    seq_dft_np, chan_cs_np = _constants(seq, hd)
    seq_dft = jnp.asarray(seq_dft_np).astype(BF16)
    mcs = _head_dft(jnp.asarray(chan_cs_np), fourier_w)

    def gain(g):
        return g.reshape(n_layers, 1, g.shape[-1])

    f1_pre, f1_post = gain(ffn1_pre_g), gain(ffn1_post_g)
    f2_pre, f2_post = gain(ffn2_pre_g), gain(ffn2_post_g)
    mix_pre, mix_post, pscale = gain(mix_pre_g), gain(mix_post_g), gain(pool_scale)
    w1 = (ffn1_w_gate, ffn1_w_up, ffn1_w_down)
    w2 = (ffn2_w_gate, ffn2_w_up, ffn2_w_down)
    win_b = w_in.astype(BF16)
    wout_b = w_out.astype(BF16)
    pw_b = pool_w.astype(BF16)

    x2 = x.reshape(n_b * seq, d)
    h2 = None
    for l in range(n_layers):
        x2, h2 = _ffn(h2, x2, f1_pre, *w1, f1_post, mix_pre, l, l)
        pq, yp = _mix_in(h2.reshape(n_b, seq, d), win_b, mcs, pw_b, pscale, l)
        x3 = _mix_out(x2.reshape(n_b, seq, d), seq_dft, pq, yp, wout_b, mix_post, l)
        x2 = x3.reshape(n_b * seq, d)
        if l + 1 < n_layers:
            x2, h2 = _ffn(None, x2, f2_pre, *w2, f2_post, f1_pre, l, l + 1)
        else:
            x2 = _ffn(None, x2, f2_pre, *w2, f2_post, None, l, None)
    return x2.reshape(n_b, seq, d)
```

```python
import functools

import numpy as np
import jax
import jax.numpy as jnp
from jax import lax
from jax.experimental import pallas as pl
from jax.experimental.pallas import tpu as pltpu

EPS = 1e-6
N_FOURIER_HEADS = 4
POOL_WINDOWS = (2, 4, 8, 16)

V7X_SUBLANES = 8
V7X_LANES = 128
BF16_TILE_ROWS = 2 * V7X_SUBLANES
V7X_VMEM_LIMIT_BYTES = 58 * 1024 * 1024

FFN_TOKEN_TILE = 512
FFN_HIDDEN_CHUNK = 256
FFN_WEIGHT_LOAD_STEPS = 4
NORM_ROW_CHUNK = 128
MIX_ROW_CHUNK = 512
DFT_RADIX = 2
MIX_OUT_ROW_TILE = 1024
POOL_PAD_ROWS = 2 * V7X_SUBLANES

BF16 = jnp.bfloat16
F32 = jnp.float32


def _rms_scale(x):
    return lax.rsqrt(jnp.mean(x * x, axis=-1, keepdims=True) + EPS)


def _residual_and_next_norm(y, x, g_post, g_next):
    x_new = x + y * _rms_scale(y) * g_post
    h_new = None if g_next is None else (x_new * _rms_scale(x_new) * g_next).astype(BF16)
    return x_new, h_new


def _exact_zero_from(v):
    bits = pltpu.bitcast(v, jnp.uint32)
    half_word = jnp.uint32(16)
    zero = lax.shift_right_logical(lax.shift_right_logical(bits, half_word), half_word)
    return pltpu.bitcast(zero, F32)


def _gain_spec(layer, width):
    return pl.BlockSpec((None, 1, width), lambda i: (layer, 0, 0))


def _ffn_body(*refs, n_tiles, emit_h, pre_norm):
    refs = list(refs)
    hin_ref, x_ref = refs[0], refs[1]
    refs = refs[2:]
    gpre_ref = refs.pop(0) if pre_norm else None
    wg_ref, wu_ref, wd_ref, gpost_ref = refs[:4]
    refs = refs[4:]
    gnext_ref = refs.pop(0) if emit_h else None
    xo_ref = refs.pop(0)
    ho_ref = refs.pop(0) if emit_h else None
    wg_s, wu_s, wd_s, g_scr, y_scr = refs[:5]
    h_bufs = tuple(refs[5:7]) if pre_norm else None
    tm = x_ref.shape[0]
    d_ff = wg_s.shape[1]
    step = pl.program_id(0)
    n_load = FFN_WEIGHT_LOAD_STEPS
    i = step - n_load

    @pl.when(step < n_load)
    def _():
        for w_ref, w_s in ((wg_ref, wg_s), (wu_ref, wu_s), (wd_ref, wd_s)):
            rows = w_ref.shape[0]
            r0 = pl.multiple_of(step * rows, rows)
            w_s[pl.ds(r0, rows), :] = w_ref[...].astype(BF16)

    def pre_norm_into(h_dst, src_ref):
        g_pre = gpre_ref[...]
        done = []
        for c in range(tm // NORM_ROW_CHUNK):
            rows = slice(c * NORM_ROW_CHUNK, (c + 1) * NORM_ROW_CHUNK)
            x = src_ref[rows, :]
            h = x * _rms_scale(x) * g_pre
            h_dst[rows, :] = h.astype(BF16)
            done.append(h[0:BF16_TILE_ROWS, 0:V7X_LANES])
        return done

    def finish_previous_tile():
        g_post = 0.5 * gpost_ref[...]
        g_next = gnext_ref[...] if emit_h else None
        done = []
        for c in range(tm // NORM_ROW_CHUNK):
            rows = slice(c * NORM_ROW_CHUNK, (c + 1) * NORM_ROW_CHUNK)
            x_new, h_new = _residual_and_next_norm(y_scr[rows, :], x_ref[rows, :], g_post, g_next)
            xo_ref[rows, :] = x_new
            if emit_h:
                ho_ref[rows, :] = h_new
            done.append(x_new[0:BF16_TILE_ROWS, 0:V7X_LANES])
        return done

    def matmuls(h_ref, done):
        tie_every = max((d_ff // FFN_HIDDEN_CHUNK) // len(done), 1)
        for c in range(d_ff // FFN_HIDDEN_CHUNK):
            sl = slice(c * FFN_HIDDEN_CHUNK, (c + 1) * FFN_HIDDEN_CHUNK)
            a = jnp.dot(h_ref[...], wg_s[:, sl], preferred_element_type=F32)
            b = jnp.dot(h_ref[...], wu_s[:, sl], preferred_element_type=F32)
            g_scr[:, sl] = ((a / (1.0 + jnp.exp(-a))) * b).astype(BF16)
            if c % tie_every == tie_every - 1 and c // tie_every < len(done):
                corner = (slice(0, BF16_TILE_ROWS),
                          slice(c * FFN_HIDDEN_CHUNK, c * FFN_HIDDEN_CHUNK + V7X_LANES))
                g_scr[corner] = g_scr[corner] + _exact_zero_from(done[c // tie_every]).astype(BF16)
        y_scr[...] = jnp.dot(g_scr[...], wd_s[...], preferred_element_type=F32)

    @pl.when(i == 0)
    def _():
        y_scr[...] = jnp.zeros(y_scr.shape, F32)
        if pre_norm:
            pre_norm_into(h_bufs[0], x_ref)

    if pre_norm:
        for parity in range(2):
            @pl.when(jnp.logical_and(jnp.logical_and(i >= 0, i < n_tiles), i % 2 == parity))
            def _(parity=parity):
                done = finish_previous_tile() + pre_norm_into(h_bufs[1 - parity], hin_ref)
                matmuls(h_bufs[parity], done)
    else:
        @pl.when(jnp.logical_and(i >= 0, i < n_tiles))
        def _():
            matmuls(hin_ref, finish_previous_tile())

    @pl.when(i == n_tiles)
    def _():
        finish_previous_tile()


def _ffn(h2d, x2d, gpre, wg, wu, wd, gpost, gnext, layer, next_layer):
    n_tok, d = x2d.shape
    d_ff = wg.shape[-1]
    tm = FFN_TOKEN_TILE
    n_tiles = n_tok // tm
    emit_h = gnext is not None
    pre_norm = h2d is None
    n_load = FFN_WEIGHT_LOAD_STEPS

    def tile(offset):
        return lambda s: (jnp.clip(s - n_load + offset, 0, n_tiles - 1), 0)

    prev = tile(-1)

    def weight_chunk(s):
        return (layer, jnp.minimum(s, n_load - 1), 0)

    in_specs = [pl.BlockSpec((tm, d), tile(1 if pre_norm else 0)), pl.BlockSpec((tm, d), prev)]
    args = [x2d if pre_norm else h2d, x2d]
    if pre_norm:
        in_specs.append(_gain_spec(layer, d))
        args.append(gpre)
    in_specs += [
        pl.BlockSpec((None, d // n_load, d_ff), weight_chunk),
        pl.BlockSpec((None, d // n_load, d_ff), weight_chunk),
        pl.BlockSpec((None, d_ff // n_load, d), weight_chunk),
        _gain_spec(layer, d),
    ]
    args += [wg, wu, wd, gpost]
    out_shape = [jax.ShapeDtypeStruct((n_tok, d), F32)]
    out_specs = [pl.BlockSpec((tm, d), prev)]
    if emit_h:
        in_specs.append(_gain_spec(next_layer, d))
        args.append(gnext)
        out_shape.append(jax.ShapeDtypeStruct((n_tok, d), BF16))
        out_specs.append(pl.BlockSpec((tm, d), prev))
    scratch_shapes = [pltpu.VMEM((d, d_ff), BF16), pltpu.VMEM((d, d_ff), BF16),
                      pltpu.VMEM((d_ff, d), BF16),
                      pltpu.VMEM((tm, d_ff), BF16), pltpu.VMEM((tm, d), F32)]
    if pre_norm:
        scratch_shapes += [pltpu.VMEM((tm, d), BF16), pltpu.VMEM((tm, d), BF16)]
    out = pl.pallas_call(
        functools.partial(_ffn_body, n_tiles=n_tiles, emit_h=emit_h, pre_norm=pre_norm),
        out_shape=tuple(out_shape),
        grid=(n_load + n_tiles + 1,),
        in_specs=in_specs,
        out_specs=tuple(out_specs),
        scratch_shapes=scratch_shapes,
        compiler_params=pltpu.CompilerParams(
            dimension_semantics=("arbitrary",), vmem_limit_bytes=V7X_VMEM_LIMIT_BYTES),
        name="ffn",
    )(*args)
    return out if emit_h else out[0]


def _head_dft_body(cs_ref, fw_ref, o_ref):
    n_layers, n_heads, hd, _ = fw_ref.shape
    for l in range(n_layers):
        for h in range(n_heads):
            fw = fw_ref[l, h]
            o_ref[l, h, :, :hd] = jnp.dot(cs_ref[0], fw, preferred_element_type=F32,
                                          precision=lax.Precision.HIGHEST).astype(BF16)
            o_ref[l, h, :, hd:] = jnp.dot(cs_ref[1], fw, preferred_element_type=F32,
                                          precision=lax.Precision.HIGHEST).astype(BF16)


def _head_dft(chan_cs, fourier_w):
    n_layers, n_heads, hd, _ = fourier_w.shape
    return pl.pallas_call(
        _head_dft_body,
        out_shape=jax.ShapeDtypeStruct((n_layers, n_heads, hd, 2 * hd), BF16),
        grid=(1,),
        in_specs=[
            pl.BlockSpec((2, hd, hd), lambda i: (0, 0, 0)),
            pl.BlockSpec((n_layers, n_heads, hd, hd), lambda i: (0, 0, 0, 0)),
        ],
        out_specs=pl.BlockSpec((n_layers, n_heads, hd, 2 * hd), lambda i: (0, 0, 0, 0)),
        compiler_params=pltpu.CompilerParams(dimension_semantics=("arbitrary",)),
        name="head_dft",
    )(chan_cs, fourier_w)


def _mix_in_body(h_ref, win_f32_ref, mcs_ref, pw_ref, pscale_ref, pq_ref, yp_ref,
                 win_ref, pad_scr, lvl_scr):
    seq, _ = h_ref.shape

    @pl.when(pl.program_id(0) == 0)
    def _():
        win_ref[...] = win_f32_ref[...].astype(BF16)

    d_f = pq_ref.shape[-1]
    hd = d_f // N_FOURIER_HEADS
    d_p = yp_ref.shape[-1]
    gd = d_p // len(POOL_WINDOWS)
    rc = MIX_ROW_CHUNK
    pad = POOL_PAD_ROWS
    lo, hi = pad // 2, pad + seq + pad // 2

    pad_scr[0:pad, :] = jnp.zeros((pad, d_p), F32)
    pad_scr[pad + seq:pad + seq + pad, :] = jnp.zeros((pad, d_p), F32)
    for buf in range(2):
        lvl_scr[buf, hi:hi + pad // 2, :] = jnp.zeros((pad // 2, d_p), F32)

    half = seq // 2
    for ci in range(half // rc):
        r0 = ci * rc
        pq_halves = []
        for base in (r0, half + r0):
            u = jnp.dot(h_ref[base:base + rc, :], win_ref[...], preferred_element_type=F32)
            pad_scr[pad + base:pad + base + rc, :] = u[:, d_f:]
            pq_halves.append([
                jnp.dot(u[:, hh * hd:(hh + 1) * hd].astype(BF16), mcs_ref[hh],
                        preferred_element_type=F32)
                for hh in range(N_FOURIER_HEADS)])
        for hh in range(N_FOURIER_HEADS):
            cols = slice(hh * hd, (hh + 1) * hd)
            lo_pq, hi_pq = pq_halves[0][hh], pq_halves[1][hh]
            for c, a in enumerate((lo_pq + hi_pq, lo_pq - hi_pq)):
                pq_ref[c, r0:r0 + rc, cols] = a[:, :hd].astype(BF16)
                pq_ref[c, half + r0:half + r0 + rc, cols] = a[:, hd:].astype(BF16)

    for g, win in enumerate(POOL_WINDOWS):
        radius = win // 2
        cols = slice(g * gd, (g + 1) * gd)
        fwd, width, buf = pad_scr, 1, 0
        while width < radius:
            src = fwd if width == 1 else fwd.at[1 - buf]
            lvl_scr[buf, lo:hi, cols] = src[lo:hi, cols] + src[lo + width:hi + width, cols]
            fwd, width, buf = lvl_scr, 2 * width, 1 - buf
        src = fwd if width == 1 else fwd.at[1 - buf]
        assert radius <= V7X_SUBLANES
        inv_mid = jnp.full((rc - 2 * V7X_SUBLANES, gd), 1.0 / (2 * radius + 1), F32)

        def inv_edge(t0):
            t = t0 + lax.broadcasted_iota(jnp.int32, (V7X_SUBLANES, gd), 0)
            count = jnp.minimum(t + radius, seq - 1) - jnp.maximum(t - radius, 0) + 1
            return 1.0 / count.astype(F32)

        for ci in range(seq // rc):
            p0 = pad + ci * rc
            total = (src[p0 - radius:p0 - radius + rc, cols] + src[p0:p0 + rc, cols]
                     + pad_scr[p0 + radius:p0 + radius + rc, cols])
            inv_count = jnp.concatenate(
                [inv_edge(ci * rc), inv_mid, inv_edge((ci + 1) * rc - V7X_SUBLANES)], axis=0)
            diff = total * inv_count - pad_scr[p0:p0 + rc, cols]
            y = jnp.dot(diff.astype(BF16), pw_ref[g], preferred_element_type=F32)
            yp_ref[ci * rc:(ci + 1) * rc, cols] = (y * pscale_ref[:, cols]).astype(BF16)


def _mix_in(h3d, win, mcs, pool_w, pool_scale, layer):
    n_b, seq, d = h3d.shape
    d_mix = win.shape[-1]
    n_heads, hd = mcs.shape[1], mcs.shape[2]
    d_f = n_heads * hd
    d_p = d_mix - d_f
    n_groups, gd = pool_w.shape[1], pool_w.shape[2]
    return pl.pallas_call(
        _mix_in_body,
        out_shape=(jax.ShapeDtypeStruct((n_b, 2, seq, d_f), BF16),
                   jax.ShapeDtypeStruct((n_b, seq, d_p), BF16)),
        grid=(n_b,),
        in_specs=[
            pl.BlockSpec((None, seq, d), lambda b: (b, 0, 0)),
            pl.BlockSpec((None, d, d_mix), lambda b: (layer, 0, 0)),
            pl.BlockSpec((None, n_heads, hd, 2 * hd), lambda b: (layer, 0, 0, 0)),
            pl.BlockSpec((None, n_groups, gd, gd), lambda b: (layer, 0, 0, 0)),
            _gain_spec(layer, d_p),
        ],
        out_specs=(pl.BlockSpec((None, 2, seq, d_f), lambda b: (b, 0, 0, 0)),
                   pl.BlockSpec((None, seq, d_p), lambda b: (b, 0, 0))),
        scratch_shapes=[pltpu.VMEM((d, d_mix), BF16),
                        pltpu.VMEM((seq + 2 * POOL_PAD_ROWS, d_p), F32),
                        pltpu.VMEM((2, seq + 2 * POOL_PAD_ROWS, d_p), F32)],
        compiler_params=pltpu.CompilerParams(
            dimension_semantics=("arbitrary",), vmem_limit_bytes=V7X_VMEM_LIMIT_BYTES),
        name="mix_in",
    )(h3d, win, mcs, pool_w, pool_scale)


def _mix_out_body(ab_ref, dft_ref, yp_ref, x_ref, wout_f32_ref, gpost_ref,
                  xo_ref, wout_ref, il_scr, cat_scr, y_scr, *, n_steps):
    tr, d = x_ref.shape
    d_f = ab_ref.shape[-1]
    n_slabs = il_scr.shape[0]
    slab = il_scr.shape[-1]
    t = pl.program_id(0)

    def finish_previous_step():
        g_post = gpost_ref[...]
        done = []
        for c in range(tr // NORM_ROW_CHUNK):
            rows = slice(c * NORM_ROW_CHUNK, (c + 1) * NORM_ROW_CHUNK)
            x_new, _ = _residual_and_next_norm(y_scr[rows, :], x_ref[rows, :], g_post, None)
            xo_ref[rows, :] = x_new
            done.append(x_new[0:BF16_TILE_ROWS, 0:V7X_LANES])
        return done

    def matmuls(done):
        j0 = pl.multiple_of((t % 2) * (tr // 2), tr // 2)
        for c in range(2):
            yf = jnp.dot(dft_ref[c, pl.ds(j0, tr // 2), :], ab_ref[c],
                         preferred_element_type=F32)
            for s in range(n_slabs):
                il_scr[s, pl.ds(c, tr // 2, stride=2), :] = yf[:, s * slab:(s + 1) * slab]
        for s in range(n_slabs):
            cat_scr[:, s * slab:(s + 1) * slab] = il_scr[s].astype(BF16)
        cat_scr[:, d_f:] = yp_ref[...]
        for c, x_done in enumerate(done[:cat_scr.shape[1] // V7X_LANES]):
            corner = (slice(0, BF16_TILE_ROWS), slice(c * V7X_LANES, (c + 1) * V7X_LANES))
            cat_scr[corner] = cat_scr[corner] + _exact_zero_from(x_done).astype(BF16)
        y_scr[...] = jnp.dot(cat_scr[...], wout_ref[...], preferred_element_type=F32)

    @pl.when(t == 0)
    def _():
        y_scr[...] = jnp.zeros(y_scr.shape, F32)
        wout_ref[...] = wout_f32_ref[...].astype(BF16)

    @pl.when(t < n_steps)
    def _():
        matmuls(finish_previous_step())

    @pl.when(t == n_steps)
    def _():
        finish_previous_step()


def _mix_out(x3d, dft2, ab, yp, wout, gpost, layer):
    n_b, seq, d = x3d.shape
    d_f = ab.shape[-1]
    d_p = yp.shape[-1]
    tr = MIX_OUT_ROW_TILE
    tiles_per_seq = seq // tr
    assert tiles_per_seq == 2, "the step parity selects the half of the class rows"
    n_steps = n_b * tiles_per_seq

    def cur(t):
        s = jnp.minimum(t, n_steps - 1)
        return s // tiles_per_seq, s % tiles_per_seq

    def prev(t):
        s = jnp.maximum(t - 1, 0)
        return s // tiles_per_seq, s % tiles_per_seq

    def x_map(t):
        b, jh = prev(t)
        return (b, jh, 0)

    return pl.pallas_call(
        functools.partial(_mix_out_body, n_steps=n_steps),
        out_shape=jax.ShapeDtypeStruct((n_b, seq, d), F32),
        grid=(n_steps + 1,),
        in_specs=[
            pl.BlockSpec((None, 2, seq, d_f), lambda t: (cur(t)[0], 0, 0, 0)),
            pl.BlockSpec((2, seq // 2, seq), lambda t: (0, 0, 0)),
            pl.BlockSpec((None, tr, d_p), lambda t: (cur(t)[0], cur(t)[1], 0)),
            pl.BlockSpec((None, tr, d), x_map),
            pl.BlockSpec((None, d_f + d_p, d), lambda t: (layer, 0, 0)),
            _gain_spec(layer, d),
        ],
        out_specs=pl.BlockSpec((None, tr, d), x_map),
        scratch_shapes=[pltpu.VMEM((d_f + d_p, d), BF16),
                        pltpu.VMEM((d_f // V7X_LANES, tr, V7X_LANES), F32),
                        pltpu.VMEM((tr, d_f + d_p), BF16),
                        pltpu.VMEM((tr, d), F32)],
        compiler_params=pltpu.CompilerParams(
            dimension_semantics=("arbitrary",), vmem_limit_bytes=V7X_VMEM_LIMIT_BYTES),
        name="mix_out",
    )(ab, dft2, yp, x3d, wout, gpost)


def _dft_cos_sin(n):
    k = np.arange(n, dtype=np.int64)
    ang = 2.0 * np.pi * ((k[:, None] * k[None, :]) % n).astype(np.float64) / n
    return np.cos(ang), np.sin(ang)


@functools.lru_cache(maxsize=None)
def _constants(seq, hd):
    nq = seq // DFT_RADIX
    n2 = np.arange(nq, dtype=np.int64)[None, None, :]
    k = (DFT_RADIX * np.arange(nq, dtype=np.int64)[None, :, None]
         + np.arange(DFT_RADIX, dtype=np.int64)[:, None, None])
    ang = 2.0 * np.pi * ((n2 * k) % seq).astype(np.float64) / seq
    seq_dft = np.concatenate([np.cos(ang), -np.sin(ang)], axis=2).astype(np.float32)
    cc, sc = _dft_cos_sin(hd)
    ortho = 1.0 / np.sqrt(float(seq) * float(hd))
    chan_cs = (np.stack([cc, sc]) * ortho).astype(np.float32)
    return seq_dft, chan_cs


def kernel(x, ffn1_pre_g, ffn1_w_gate, ffn1_w_up, ffn1_w_down, ffn1_post_g, mix_pre_g, w_in, fourier_w, pool_w, pool_scale, w_out, mix_post_g, ffn2_pre_g, ffn2_w_gate, ffn2_w_up, ffn2_w_down, ffn2_post_g):
    n_b, seq, d = x.shape
    n_layers = w_in.shape[0]
    hd = fourier_w.shape[-1]
    seq_dft_np, chan_cs_np = _constants(seq, hd)
    seq_dft = jnp.asarray(seq_dft_np).astype(BF16)
    mcs = _head_dft(jnp.asarray(chan_cs_np), fourier_w)

    def gain(g):
        return g.reshape(n_layers, 1, g.shape[-1])

    f1_pre, f1_post = gain(ffn1_pre_g), gain(ffn1_post_g)
    f2_pre, f2_post = gain(ffn2_pre_g), gain(ffn2_post_g)
    mix_pre, mix_post, pscale = gain(mix_pre_g), gain(mix_post_g), gain(pool_scale)
    w1 = (ffn1_w_gate, ffn1_w_up, ffn1_w_down)
    w2 = (ffn2_w_gate, ffn2_w_up, ffn2_w_down)
    pw_b = pool_w.astype(BF16)

    x2 = x.reshape(n_b * seq, d)
    h2 = None
    for l in range(n_layers):
        x2, h2 = _ffn(h2, x2, f1_pre, *w1, f1_post, mix_pre, l, l)
        pq, yp = _mix_in(h2.reshape(n_b, seq, d), w_in, mcs, pw_b, pscale, l)
        x3 = _mix_out(x2.reshape(n_b, seq, d), seq_dft, pq, yp, w_out, mix_post, l)
        x2 = x3.reshape(n_b * seq, d)
        if l + 1 < n_layers:
            x2, h2 = _ffn(None, x2, f2_pre, *w2, f2_post, f1_pre, l, l + 1)
        else:
            x2 = _ffn(None, x2, f2_pre, *w2, f2_post, None, l, None)
    return x2.reshape(n_b, seq, d)
```
